```python
import jax
import jax.numpy as jnp
from jax import lax
import numpy as np

D_MODEL = 1024
BATCH = 4
SEQ = 8192
DEPTH = 2
DEC_BATCH = 32
DEC_SEQ = 64
PAST_LEN = 1024

CHUNK = 64
HEAD_DIM = 64
A_WIDTH = D_MODEL // 4
A_HEADS = A_WIDTH // HEAD_DIM
A_BLOCK = 128
B_WIDTH = D_MODEL // 2
B_HEADS = B_WIDTH // HEAD_DIM
B_KV_HEADS = 2
B_GROUP = B_HEADS // B_KV_HEADS
WINDOW = 128
WIN_CHUNKS = WINDOW // CHUNK
ROT_DIM = HEAD_DIM // 4
ROPE_THETA = 500000.0
ATTN_SCALE = HEAD_DIM ** -0.5
C_WIDTH = D_MODEL // 4
C_HEADS = C_WIDTH // HEAD_DIM
MIX_WIDTH = A_WIDTH + B_WIDTH + C_WIDTH
D_FF = ((8 * D_MODEL // 3 + 255) // 256) * 256
CONV_W = 3
RMS_EPS = 1e-6

OFF_A_U = 0
OFF_A_V = OFF_A_U + A_WIDTH
OFF_B_Q = OFF_A_V + A_WIDTH
OFF_B_K = OFF_B_Q + B_HEADS * HEAD_DIM
OFF_B_V = OFF_B_K + B_KV_HEADS * HEAD_DIM
OFF_C_Q = OFF_B_V + B_KV_HEADS * HEAD_DIM
OFF_C_K = OFF_C_Q + C_WIDTH
OFF_C_V = OFF_C_K + C_WIDTH
OFF_C_O = OFF_C_V + C_WIDTH
OFF_C_G = OFF_C_O + C_WIDTH
IN_WIDTH = OFF_C_G + 2 * C_HEADS

kernel_name = 'hybrid_streaming_encoder_step'


def rmsnorm(x, g):
    xf = x.astype(jnp.float32)
    xf = xf * lax.rsqrt(jnp.mean(xf * xf, axis=-1, keepdims=True) + RMS_EPS)
    return xf.astype(x.dtype) * g


def partial_rotary(x, pos):
    half = ROT_DIM // 2
    inv_freq = ROPE_THETA ** (-jnp.arange(half, dtype=jnp.float32) / half)
    ang = pos.astype(jnp.float32)[:, None] * inv_freq[None, :]
    cos = jnp.cos(ang)[:, None, :]
    sin = jnp.sin(ang)[:, None, :]
    x1 = x[..., :half].astype(jnp.float32)
    x2 = x[..., half:ROT_DIM].astype(jnp.float32)
    rot = jnp.concatenate([x1 * cos - x2 * sin, x2 * cos + x1 * sin], axis=-1).astype(x.dtype)
    return jnp.concatenate([rot, x[..., ROT_DIM:]], axis=-1)


def project(x, pos, norm_g, w_in, gmlp_norm_g, gate_b):
    B_, T, _ = x.shape
    z = rmsnorm(x, norm_g) @ w_in

    def heads(off, n):
        return z[..., off:off + n * HEAD_DIM].reshape(B_, T, n, HEAD_DIM)

    a_u = jax.nn.gelu(heads(OFF_A_U, A_HEADS), approximate=False)
    a_v = rmsnorm(jax.nn.gelu(z[..., OFF_A_V:OFF_A_V + A_WIDTH], approximate=False), gmlp_norm_g)
    a_v = a_v.reshape(B_, T, A_HEADS, HEAD_DIM)
    b_q = partial_rotary(heads(OFF_B_Q, B_HEADS), pos)
    b_k = partial_rotary(heads(OFF_B_K, B_KV_HEADS), pos)
    b_v = heads(OFF_B_V, B_KV_HEADS)
    c_q = heads(OFF_C_Q, C_HEADS)
    c_k = heads(OFF_C_K, C_HEADS) * ATTN_SCALE
    c_v = heads(OFF_C_V, C_HEADS)
    c_o = jax.nn.sigmoid(z[..., OFF_C_O:OFF_C_O + C_WIDTH])
    gates = z[..., OFF_C_G:OFF_C_G + 2 * C_HEADS].astype(jnp.float32) + gate_b.astype(jnp.float32)
    c_logi = gates[..., :C_HEADS]
    c_logf = jax.nn.log_sigmoid(gates[..., C_HEADS:])
    return a_u, a_v, b_q, b_k, b_v, c_q, c_k, c_v, c_o, c_logi, c_logf


def gmlp_mask():
    i = jnp.arange(A_BLOCK)
    return (i[None, :] // CHUNK) <= (i[:, None] // CHUNK)


def gmlp_prompt(u, v, w_s, b_s):
    B_, S = v.shape[:2]
    w = jnp.where(gmlp_mask()[None], w_s, 0.0)
    vb = v.reshape(B_, S // A_BLOCK, A_BLOCK, A_HEADS, HEAD_DIM)
    mixed = jnp.einsum('hij,bgjhd->bgihd', w, vb) + b_s.T[:, :, None]
    return u * mixed.reshape(B_, S, A_HEADS, HEAD_DIM)


def gmlp_sample(u, v, w_s, b_s):
    T = v.shape[1]
    w = jnp.where(gmlp_mask()[None], w_s, 0.0)[:, :T, :T]
    mixed = jnp.einsum('hij,bjhd->bihd', w, v) + b_s[:, :T].T[:, :, None]
    return u * mixed


def sink_softmax(s, sinks):
    sk = jnp.broadcast_to(sinks.reshape(B_KV_HEADS, B_GROUP, 1, 1).astype(jnp.float32), s.shape[:-1] + (1,))
    p = jax.nn.softmax(jnp.concatenate([s, sk], axis=-1), axis=-1)
    return p[..., :-1]


def swa_prompt(q, k, v, sinks):
    B_, S = q.shape[:2]
    nC = S // CHUNK
    qc = q.reshape(B_, nC, CHUNK, B_KV_HEADS, B_GROUP, HEAD_DIM)
    pad = ((0, 0), (WIN_CHUNKS * CHUNK, 0), (0, 0), (0, 0))
    kp = jnp.pad(k, pad).reshape(B_, nC + WIN_CHUNKS, CHUNK, B_KV_HEADS, HEAD_DIM)
    vp = jnp.pad(v, pad).reshape(B_, nC + WIN_CHUNKS, CHUNK, B_KV_HEADS, HEAD_DIM)
    kb = jnp.concatenate([kp[:, j:j + nC] for j in range(WIN_CHUNKS + 1)], axis=2)
    vb = jnp.concatenate([vp[:, j:j + nC] for j in range(WIN_CHUNKS + 1)], axis=2)
    s = jnp.einsum('bcqkgd,bcskd->bckgqs', qc, kb).astype(jnp.float32) * ATTN_SCALE
    key_pos = (jnp.arange(nC)[:, None] - WIN_CHUNKS) * CHUNK + jnp.arange((WIN_CHUNKS + 1) * CHUNK)[None, :]
    s = jnp.where((key_pos >= 0)[None, :, None, None, None, :], s, -jnp.inf)
    p = sink_softmax(s, sinks)
    o = jnp.einsum('bckgqs,bcskd->bcqkgd', p.astype(vb.dtype), vb)
    return o.reshape(B_, S, B_WIDTH)


def swa_sample(q, k_new, v_new, win_k, win_v, sinks):
    B_, T = q.shape[:2]
    kk = jnp.concatenate([win_k.astype(k_new.dtype), k_new], axis=1)
    vv = jnp.concatenate([win_v.astype(v_new.dtype), v_new], axis=1)
    qg = q.reshape(B_, T, B_KV_HEADS, B_GROUP, HEAD_DIM)
    s = jnp.einsum('bqkgd,bskd->bkgqs', qg, kk).astype(jnp.float32) * ATTN_SCALE
    p = sink_softmax(s, sinks)
    o = jnp.einsum('bkgqs,bskd->bqkgd', p.astype(vv.dtype), vv)
    return o.reshape(B_, T, B_WIDTH)


def mlstm_chunk(carry, inp):
    C, n, m = carry
    q, k, v, logi, logf = inp
    L = q.shape[2]
    b = jnp.cumsum(logf, axis=-1)
    causal = jnp.tril(jnp.ones((L, L), dtype=bool))
    D = jnp.where(causal, b[..., :, None] - b[..., None, :] + logi[..., None, :], -jnp.inf)
    inter = b + m[..., None]
    m_t = jnp.maximum(inter, jnp.max(D, axis=-1))
    w_inter = jnp.exp(inter - m_t)
    S = jnp.einsum('bhtd,bhsd->bhts', q, k) * jnp.exp(D - m_t[..., None])
    num = w_inter[..., None] * jnp.einsum('bhvk,bhtk->bhtv', C, q) + jnp.einsum('bhts,bhsv->bhtv', S, v)
    den = w_inter * jnp.einsum('bhk,bhtk->bht', n, q) + jnp.sum(S, axis=-1)
    h = num / jnp.maximum(jnp.abs(den), jnp.exp(-m_t))[..., None]
    m_new = m_t[..., -1]
    decay = jnp.exp(b[..., -1] + m - m_new)
    w_s = jnp.exp(b[..., -1:] - b + logi - m_new[..., None])
    C_new = decay[..., None, None] * C + jnp.einsum('bhs,bhsv,bhsk->bhvk', w_s, v, k)
    n_new = decay[..., None] * n + jnp.einsum('bhs,bhsk->bhk', w_s, k)
    return (C_new, n_new, m_new), h


def mlstm(q, k, v, logi, logf, carry, chunk):
    B_, T, H, D = q.shape
    nC = T // chunk

    def blocks(a):
        a = a.astype(jnp.float32).reshape((B_, nC, chunk) + a.shape[2:])
        return jnp.moveaxis(a, (1, 3), (0, 2))

    carry = (carry[0].astype(jnp.float32), carry[1].astype(jnp.float32), carry[2].astype(jnp.float32))
    carry, h = lax.scan(mlstm_chunk, carry, (blocks(q), blocks(k), blocks(v), blocks(logi), blocks(logf)))
    h = jnp.transpose(h, (1, 0, 3, 2, 4)).reshape(B_, T, H * D)
    return h, carry


def conv_ffn(x, g, w_up, conv_w, conv_b, w_down, conv_buf):
    up = rmsnorm(x, g) @ w_up
    T = up.shape[1]
    full = jnp.concatenate([conv_buf.astype(up.dtype), up], axis=1)
    c = conv_b + full[:, 0:T] * conv_w[0]
    for j in range(1, CONV_W):
        c = c + full[:, j:j + T] * conv_w[j]
    y = (jax.nn.gelu(c[..., :D_FF], approximate=False) * c[..., D_FF:]) @ w_down
    return y, full[:, -(CONV_W - 1):]


def layer(x, pos, lp, cache):
    B_, T, _ = x.shape
    a_u, a_v, b_q, b_k, b_v, c_q, c_k, c_v, c_o, c_logi, c_logf = project(
        x, pos, lp['attn_norm'], lp['w_in'], lp['gmlp_norm'], lp['mlstm_gate_b'])
    if cache is None:
        a_out = gmlp_prompt(a_u, a_v, lp['gmlp_w'], lp['gmlp_b'])
        b_out = swa_prompt(b_q, b_k, b_v, lp['attn_sinks'])
        carry0 = (jnp.zeros((B_, C_HEADS, HEAD_DIM, HEAD_DIM), jnp.float32),
                  jnp.zeros((B_, C_HEADS, HEAD_DIM), jnp.float32),
                  jnp.zeros((B_, C_HEADS), jnp.float32))
        c_h, (C, n, m) = mlstm(c_q, c_k, c_v, c_logi, c_logf, carry0, CHUNK)
        conv_buf = jnp.zeros((B_, CONV_W - 1, 2 * D_FF), x.dtype)
    else:
        win_k, win_v, C0, n0, m0, conv_buf = cache
        a_out = gmlp_sample(a_u, a_v, lp['gmlp_w'], lp['gmlp_b'])
        b_out = swa_sample(b_q, b_k, b_v, win_k, win_v, lp['attn_sinks'])
        c_h, (C, n, m) = mlstm(c_q, c_k, c_v, c_logi, c_logf, (C0, n0, m0), T)
    mixed = jnp.concatenate([a_out.reshape(B_, T, A_WIDTH), b_out, (c_o * c_h).astype(x.dtype)], axis=-1)
    x = x + mixed @ lp['w_out']
    y, conv_state = conv_ffn(x, lp['ffn_norm'], lp['w_up'], lp['conv_w'], lp['conv_b'], lp['w_down'], conv_buf)
    x = x + y
    if cache is None:
        return x, (b_k[:, -WINDOW:], b_v[:, -WINDOW:], C, n, m, conv_state)
    return x, (a_v.reshape(B_, T, A_WIDTH), b_k, b_v, C, n, m, conv_state)


def setup_inputs(seed: int = 0) -> dict:
    key = jax.random.key(seed)
    ks = jax.random.split(key, 23)

    def nrm(k, shape, scale):
        return scale * jax.random.normal(k, shape, jnp.float32)

    return {
        'x_prompt': nrm(ks[0], (BATCH, SEQ, D_MODEL), 1.0),
        'x_sample': nrm(ks[1], (DEC_BATCH, DEC_SEQ, D_MODEL), 1.0),
        'cache_k': nrm(ks[2], (DEPTH, DEC_BATCH, WINDOW, B_KV_HEADS, HEAD_DIM), 1.0),
        'cache_v': nrm(ks[3], (DEPTH, DEC_BATCH, WINDOW, B_KV_HEADS, HEAD_DIM), 1.0),
        'state_C': nrm(ks[4], (DEPTH, DEC_BATCH, C_HEADS, HEAD_DIM, HEAD_DIM), 0.1),
        'state_n': nrm(ks[5], (DEPTH, DEC_BATCH, C_HEADS, HEAD_DIM), 0.1),
        'state_m': nrm(ks[6], (DEPTH, DEC_BATCH, C_HEADS), 1.0),
        'cache_conv': nrm(ks[7], (DEPTH, DEC_BATCH, CONV_W - 1, 2 * D_FF), 1.0),
        'attn_norm': 1.0 + nrm(ks[8], (DEPTH, D_MODEL), 0.02),
        'w_in': nrm(ks[9], (DEPTH, D_MODEL, IN_WIDTH), D_MODEL ** -0.5),
        'gmlp_norm': 1.0 + nrm(ks[10], (DEPTH, A_WIDTH), 0.02),
        'gmlp_w': nrm(ks[11], (DEPTH, A_HEADS, A_BLOCK, A_BLOCK), A_BLOCK ** -0.5),
        'gmlp_b': 1.0 + nrm(ks[12], (DEPTH, A_HEADS, A_BLOCK), 0.02),
        'attn_sinks': nrm(ks[13], (DEPTH, B_HEADS), 0.5),
        'mlstm_gate_b': jnp.concatenate([nrm(ks[14], (DEPTH, C_HEADS), 0.1),
                                         3.0 + nrm(ks[15], (DEPTH, C_HEADS), 0.5)], axis=-1),
        'w_out': nrm(ks[16], (DEPTH, MIX_WIDTH, D_MODEL), MIX_WIDTH ** -0.5),
        'ffn_norm': 1.0 + nrm(ks[17], (DEPTH, D_MODEL), 0.02),
        'w_up': nrm(ks[18], (DEPTH, D_MODEL, 2 * D_FF), D_MODEL ** -0.5),
        'conv_w': nrm(ks[19], (DEPTH, CONV_W, 2 * D_FF), CONV_W ** -0.5),
        'conv_b': nrm(ks[20], (DEPTH, 2 * D_FF), 0.02),
        'w_down': nrm(ks[21], (DEPTH, D_FF, D_MODEL), D_FF ** -0.5),
        'final_norm': 1.0 + nrm(ks[22], (D_MODEL,), 0.02),
    }


def reference(x_prompt, x_sample, cache_k, cache_v, state_C, state_n, state_m, cache_conv,
              attn_norm, w_in, gmlp_norm, gmlp_w, gmlp_b, attn_sinks, mlstm_gate_b, w_out,
              ffn_norm, w_up, conv_w, conv_b, w_down, final_norm):
    pos_prompt = jnp.arange(x_prompt.shape[1], dtype=jnp.int32)
    pos_sample = PAST_LEN + jnp.arange(x_sample.shape[1], dtype=jnp.int32)
    xp, xs = x_prompt, x_sample
    p_st, s_st = [], []
    for l in range(DEPTH):
        lp = {'attn_norm': attn_norm[l], 'w_in': w_in[l], 'gmlp_norm': gmlp_norm[l], 'gmlp_w': gmlp_w[l],
              'gmlp_b': gmlp_b[l], 'attn_sinks': attn_sinks[l], 'mlstm_gate_b': mlstm_gate_b[l],
              'w_out': w_out[l], 'ffn_norm': ffn_norm[l], 'w_up': w_up[l], 'conv_w': conv_w[l],
              'conv_b': conv_b[l], 'w_down': w_down[l]}
        xp, st = layer(xp, pos_prompt, lp, None)
        p_st.append(st)
        xs, st = layer(xs, pos_sample, lp,
                       (cache_k[l], cache_v[l], state_C[l], state_n[l], state_m[l], cache_conv[l]))
        s_st.append(st)
    y_prompt = rmsnorm(xp, final_norm)
    y_sample = rmsnorm(xs, final_norm)

    def stk(states, i):
        return jnp.stack([s[i] for s in states], axis=0)

    return (y_prompt, y_sample,
            stk(p_st, 0), stk(p_st, 1), stk(p_st, 2), stk(p_st, 3), stk(p_st, 4), stk(p_st, 5),
            stk(s_st, 0), stk(s_st, 1), stk(s_st, 2), stk(s_st, 3), stk(s_st, 4), stk(s_st, 5), stk(s_st, 6))
```

```python
import functools

import numpy as np
import jax
import jax.numpy as jnp
from jax import lax
from jax.experimental import pallas as pl
from jax.experimental.pallas import tpu as pltpu

HEAD_DIM = 64
CHUNK = 64
A_BLOCK = 128
ROT_DIM = 16
ROPE_THETA = 500000.0
ATTN_SCALE = HEAD_DIM ** -0.5
RMS_EPS = 1e-6
CONV_W = 3
PAST_LEN = 1024

LANES = 128
SUBLANES = 8
VMEM_LIMIT_BYTES = 56 * 1024 * 1024

PROMPT_TILE = 256
MLSTM_CHUNK = 128
SAMPLE_ROWS = 4
FFN_COLS = 256

F32 = jnp.float32
BF16 = jnp.bfloat16


def _dot(a, b):
    return jnp.dot(a, b, preferred_element_type=F32)


def _dot_nt(a, b):
    return lax.dot_general(a, b, (((1,), (1,)), ((), ())), preferred_element_type=F32)


def _dot_tn(a, b):
    return lax.dot_general(a, b, (((0,), (0,)), ((), ())), preferred_element_type=F32)


def _rmsnorm(x, g):
    ms = jnp.mean(x * x, axis=-1, keepdims=True)
    return x * lax.rsqrt(ms + RMS_EPS) * g


def _gelu(x):
    return 0.5 * x * (1.0 + lax.erf(x * np.float32(np.sqrt(0.5))))


def _log_sigmoid(x):
    return jnp.minimum(x, 0.0) - jnp.log1p(jnp.exp(-jnp.abs(x)))


class _Dims:
    def __init__(self, d_model, kv_heads):
        self.d = d_model
        self.aw = d_model // 4
        self.ah = self.aw // HEAD_DIM
        self.bw = d_model // 2
        self.bh = self.bw // HEAD_DIM
        self.kv = kv_heads
        self.grp = self.bh // kv_heads
        self.kvw = kv_heads * HEAD_DIM
        self.cw = d_model // 4
        self.ch = self.cw // HEAD_DIM
        self.off_a_u = 0
        self.off_a_v = self.off_a_u + self.aw
        self.off_b_q = self.off_a_v + self.aw
        self.off_b_k = self.off_b_q + self.bw
        self.off_b_v = self.off_b_k + self.kvw
        self.off_c_q = self.off_b_v + self.kvw
        self.off_c_k = self.off_c_q + self.cw
        self.off_c_v = self.off_c_k + self.cw
        self.off_c_o = self.off_c_v + self.cw
        self.off_c_g = self.off_c_o + self.cw
        self.in_width = self.off_c_g + 2 * self.ch
        self.in_pad = self.off_c_g + LANES
        assert self.off_c_g % LANES == 0 and 2 * self.ch <= SUBLANES


def _scan_lanes(x, length):
    outs = []
    for j in range(x.shape[1] // LANES):
        xj = x[:, j * LANES:(j + 1) * LANES]
        pos = lax.broadcasted_iota(jnp.int32, xj.shape, 1) & (length - 1)
        s = 1
        while s < length:
            xj = xj + jnp.where(pos >= s, pltpu.roll(xj, s, 1), 0.0)
            s *= 2
        outs.append(xj)
    return outs[0] if len(outs) == 1 else jnp.concatenate(outs, axis=1)


def _rope(xt, cos_t, sin_a, sin_b, nb, t):
    up = pltpu.roll(xt, LANES - ROT_DIM // 2, 1)
    dn = pltpu.roll(xt, ROT_DIM // 2, 1)
    shp = (nb, t, LANES)
    out = (xt.reshape(shp) * cos_t[None] + up.reshape(shp) * sin_a[None]
           + dn.reshape(shp) * sin_b[None])
    return out.reshape(nb * t, LANES)


def _attend(q4, kk, vv, sink_col, key_ok):
    s = _dot_nt(q4, kk)
    if key_ok is not None:
        s = jnp.where(key_ok, s, -jnp.inf)
    mx = jnp.maximum(jnp.max(s, axis=-1, keepdims=True), sink_col)
    p = jnp.exp(s - mx)
    denom = jnp.sum(p, axis=-1, keepdims=True) + jnp.exp(sink_col - mx)
    return _dot(p.astype(BF16), vv) / denom


def _mlstm_chunk(q, k, v_tile, odd, logi_col, b_col, r_row, m_state, m_prev, causal):
    L = q.shape[0]
    d = jnp.where(causal, b_col + r_row, -jnp.inf)
    inter = b_col + m_prev
    m_t = jnp.maximum(inter, jnp.max(d, axis=-1, keepdims=True))
    w_inter = jnp.exp(inter - m_t)
    s = _dot_nt(q, k) * jnp.exp(d - m_t)
    q_state = _dot(q, m_state.astype(BF16))
    vt = pltpu.roll(v_tile, HEAD_DIM, 1) if odd else v_tile
    lane = lax.broadcasted_iota(jnp.int32, (L, LANES), 1)
    v_lo = jnp.where(lane < HEAD_DIM, vt, 0.0)
    sv = _dot(s.astype(BF16), v_lo.astype(BF16))
    num = w_inter * q_state[:, :HEAD_DIM] + sv[:, :HEAD_DIM]
    den = w_inter * q_state[:, HEAD_DIM:HEAD_DIM + 1] + jnp.sum(s, axis=-1, keepdims=True)
    h = num / jnp.maximum(jnp.abs(den), jnp.exp(-m_t))
    m_new = m_t[L - 1:L, :]
    b_last = b_col[L - 1:L, :]
    decay = jnp.exp(b_last + m_prev - m_new)
    w_s = jnp.exp(b_last - b_col + logi_col - m_new)
    v_ext = jnp.where(lane < HEAD_DIM, vt * w_s, jnp.where(lane == HEAD_DIM, w_s, 0.0))
    m_state_new = decay * m_state + _dot_tn(k, v_ext.astype(BF16))
    return h, m_state_new, m_new


def _mixer_kernel(dm, prompt, nb, t, mchunk, *refs):
    if prompt:
        (x_ref, cos_ref, sa_ref, sb_ref, ng_ref, win_ref, gb_ref, gg_ref, gw_ref, gbx_ref,
         sinks_ref, wout_ref,
         x1_ref, klast_ref, vlast_ref, mst_ref, mm_ref,
         kbuf, vbuf, mixed_ref) = refs
    else:
        (x_ref, cos_ref, sa_ref, sb_ref, ng_ref, win_ref, gb_ref, gg_ref, gw_ref, gbx_ref,
         sinks_ref, wout_ref, ck_ref, cv_ref, m0_ref, mm0_ref,
         x1_ref, av_ref, knew_ref, vnew_ref, mst_ref, mm_ref,
         kbuf, vbuf, mixed_ref) = refs
    d_model = dm.d
    rows = nb * t
    win = 2 * CHUNK
    step = pl.program_id(1) if prompt else None

    x = x_ref[...].reshape(rows, d_model)
    xn = _rmsnorm(x, ng_ref[...]).astype(BF16)

    def sec(off, width):
        return _dot(xn, win_ref[:, off:off + width])

    cos_t, sin_a, sin_b = cos_ref[...], sa_ref[...], sb_ref[...]

    a_u = _gelu(sec(dm.off_a_u, dm.aw))
    a_v = _rmsnorm(_gelu(sec(dm.off_a_v, dm.aw)), gg_ref[...])
    zq = sec(dm.off_b_q, dm.bw)
    q_rot = jnp.concatenate(
        [_rope(zq[:, j * LANES:(j + 1) * LANES], cos_t, sin_a, sin_b, nb, t)
         for j in range(dm.bw // LANES)], axis=1) * ATTN_SCALE
    zk = sec(dm.off_b_k, dm.kvw)
    k_rot = jnp.concatenate(
        [_rope(zk[:, j * LANES:(j + 1) * LANES], cos_t, sin_a, sin_b, nb, t)
         for j in range(dm.kvw // LANES)], axis=1)
    v_new = sec(dm.off_b_v, dm.kvw)
    c_q = sec(dm.off_c_q, dm.cw)
    c_k = sec(dm.off_c_k, dm.cw) * ATTN_SCALE
    c_v = sec(dm.off_c_v, dm.cw)
    c_o = jax.nn.sigmoid(sec(dm.off_c_o, dm.cw))
    gates = sec(dm.off_c_g, LANES) + gb_ref[...]
    lane_g = lax.broadcasted_iota(jnp.int32, gates.shape, 1)
    gates = jnp.where(lane_g < dm.ch, gates, _log_sigmoid(gates))
    gates_t = gates.T
    cum_t = _scan_lanes(gates_t[0:SUBLANES, :], mchunk)
    cum = jnp.concatenate(
        [cum_t, jnp.zeros((LANES - SUBLANES, rows), F32)], axis=0).T

    if prompt:
        @pl.when(step == 0)
        def _():
            kbuf[:, :, 0:win, :] = jnp.zeros((nb, dm.kv, win, HEAD_DIM), BF16)
            vbuf[:, :, 0:win, :] = jnp.zeros((nb, dm.kv, win, HEAD_DIM), BF16)
            mst_ref[...] = jnp.zeros(mst_ref.shape, F32)
            mm_ref[...] = jnp.zeros(mm_ref.shape, F32)
    else:
        for bi in range(nb):
            for g in range(dm.kv):
                kbuf[bi, g, 0:win, :] = ck_ref[bi, :, g * HEAD_DIM:(g + 1) * HEAD_DIM].astype(BF16)
                vbuf[bi, g, 0:win, :] = cv_ref[bi, :, g * HEAD_DIM:(g + 1) * HEAD_DIM].astype(BF16)
    for bi in range(nb):
        for g in range(dm.kv):
            sl = slice(g * HEAD_DIM, (g + 1) * HEAD_DIM)
            kbuf[bi, g, win:win + t, :] = k_rot[bi * t:(bi + 1) * t, sl].astype(BF16)
            vbuf[bi, g, win:win + t, :] = v_new[bi * t:(bi + 1) * t, sl].astype(BF16)

    blk = min(A_BLOCK, t)
    ri = lax.broadcasted_iota(jnp.int32, (A_BLOCK, A_BLOCK), 0)
    ci = lax.broadcasted_iota(jnp.int32, (A_BLOCK, A_BLOCK), 1)
    block_causal = (ci // CHUNK) <= (ri // CHUNK)
    w_sp = [jnp.where(block_causal, gw_ref[h], 0.0)[:blk, :blk].astype(BF16) for h in range(dm.ah)]
    lane_a = lax.broadcasted_iota(jnp.int32, (blk, dm.aw), 1) // HEAD_DIM
    gbx = gbx_ref[0:blk, :]
    for r0 in range(0, rows, blk):
        vb = a_v[r0:r0 + blk].astype(BF16)
        mixed = _dot(w_sp[dm.ah - 1], vb)
        for h in range(dm.ah - 2, -1, -1):
            mixed = jnp.where(lane_a == h, _dot(w_sp[h], vb), mixed)
        mixed_ref[r0:r0 + blk, 0:dm.aw] = a_u[r0:r0 + blk] * (mixed + gbx)

    q_heads = [q_rot[:, h * HEAD_DIM:(h + 1) * HEAD_DIM] for h in range(dm.bh)]
    sink_cols = [jnp.concatenate([jnp.full((CHUNK, 1), sinks_ref[g * dm.grp + j], F32)
                                  for j in range(dm.grp)], axis=0) for g in range(dm.kv)]
    key_lane = lax.broadcasted_iota(jnp.int32, (1, win + CHUNK), 1)
    for bi in range(nb):
        for c in range(t // CHUNK):
            r0 = bi * t + c * CHUNK
            key_ok = None
            if prompt and c < win // CHUNK:
                key_ok = (step * t + c * CHUNK - win + key_lane) >= 0
            for g in range(dm.kv):
                q4 = jnp.concatenate([q_heads[g * dm.grp + j][r0:r0 + CHUNK]
                                      for j in range(dm.grp)], axis=0).astype(BF16)
                kk = kbuf[bi, g, c * CHUNK:c * CHUNK + win + CHUNK, :]
                vv = vbuf[bi, g, c * CHUNK:c * CHUNK + win + CHUNK, :]
                o = _attend(q4, kk, vv, sink_cols[g], key_ok)
                for j in range(dm.grp):
                    col = dm.aw + (g * dm.grp + j) * HEAD_DIM
                    mixed_ref[r0:r0 + CHUNK, col:col + HEAD_DIM] = o[j * CHUNK:(j + 1) * CHUNK]

    ti = lax.broadcasted_iota(jnp.int32, (mchunk, mchunk), 0)
    si = lax.broadcasted_iota(jnp.int32, (mchunk, mchunk), 1)
    causal = si <= ti
    for h in range(dm.ch):
        sl = slice(h * HEAD_DIM, (h + 1) * HEAD_DIM)
        tile = slice((h // 2) * LANES, (h // 2 + 1) * LANES)
        q_h = c_q[:, sl].astype(BF16)
        k_h = c_k[:, sl].astype(BF16)
        o_h = c_o[:, sl]
        for bi in range(nb):
            if prompt:
                m_state = mst_ref[bi, h]
                m_prev = mm_ref[bi, h:h + 1, 0:1]
            else:
                m_state = m0_ref[bi, h]
                m_prev = mm0_ref[bi, h:h + 1, 0:1]
            for c in range(t // mchunk):
                r0 = bi * t + c * mchunk
                rs = slice(r0, r0 + mchunk)
                r_row = gates_t[h:h + 1, rs] - cum_t[dm.ch + h:dm.ch + h + 1, rs]
                hh, m_state, m_prev = _mlstm_chunk(
                    q_h[rs], k_h[rs], c_v[rs, tile], h % 2 == 1,
                    gates[rs, h:h + 1], cum[rs, dm.ch + h:dm.ch + h + 1], r_row,
                    m_state, m_prev, causal)
                col = dm.aw + dm.bw + h * HEAD_DIM
                mixed_ref[rs, col:col + HEAD_DIM] = o_h[rs] * hh
            mst_ref[bi, h] = m_state
            mm_ref[bi, h:h + 1, :] = jnp.broadcast_to(m_prev, (1, LANES))

    x1 = x + _dot(mixed_ref[...].astype(BF16), wout_ref[...])
    x1_ref[...] = x1.reshape(nb, t, d_model)

    if prompt:
        @pl.when(step == pl.num_programs(1) - 1)
        def _():
            klast_ref[0] = k_rot[t - win:t]
            vlast_ref[0] = v_new[t - win:t]

        for g in range(dm.kv):
            kbuf[0, g, 0:win, :] = kbuf[0, g, t:t + win, :]
            vbuf[0, g, 0:win, :] = vbuf[0, g, t:t + win, :]
    else:
        av_ref[...] = a_v.reshape(nb, t, dm.aw)
        knew_ref[...] = k_rot.reshape(nb, t, dm.kvw)
        vnew_ref[...] = v_new.reshape(nb, t, dm.kvw)


def _const_spec(shape):
    zeros = (0,) * len(shape)
    return pl.BlockSpec(shape, lambda *_: zeros, pipeline_mode=pl.Buffered(1))


def _mixer_call(dm, prompt, x, tables, lw, caches=None):
    batch, seq, d_model = x.shape
    if prompt:
        nb, t, mchunk = 1, min(PROMPT_TILE, seq), min(MLSTM_CHUNK, seq)
        grid = (batch, seq // t)
        bmap = lambda b, s: (b, s, 0)
        smap = lambda b, s: (b, 0, 0)
        smap4 = lambda b, s: (b, 0, 0, 0)
        tmap = lambda b, s: (s, 0)
        sem = ("arbitrary", "arbitrary")
    else:
        nb, t, mchunk = min(SAMPLE_ROWS, batch), seq, seq
        grid = (batch // nb,)
        bmap = lambda b: (b, 0, 0)
        smap = bmap
        smap4 = lambda b: (b, 0, 0, 0)
        tmap = lambda b: (0, 0)
        sem = ("arbitrary",)
    assert seq % t == 0 and t % mchunk == 0 and t % CHUNK == 0 and (nb * t) % LANES == 0
    assert batch % nb == 0 and mchunk & (mchunk - 1) == 0
    win = 2 * CHUNK

    in_specs = [
        pl.BlockSpec((nb, t, d_model), bmap),
        pl.BlockSpec((t, LANES), tmap), pl.BlockSpec((t, LANES), tmap), pl.BlockSpec((t, LANES), tmap),
        _const_spec((1, d_model)),
        _const_spec((d_model, dm.in_pad)),
        _const_spec((1, LANES)),
        _const_spec((1, dm.aw)),
        _const_spec((dm.ah, A_BLOCK, A_BLOCK)),
        _const_spec((A_BLOCK, dm.aw)),
        pl.BlockSpec(memory_space=pltpu.SMEM),
        _const_spec((d_model, d_model)),
    ]
    args = [x, *tables, lw["attn_norm"], lw["w_in"], lw["gate_b"], lw["gmlp_norm"], lw["gmlp_w"],
            lw["gmlp_bx"], lw["sinks"], lw["w_out"]]
    state_shapes = [jax.ShapeDtypeStruct((batch, dm.ch, HEAD_DIM, LANES), F32),
                    jax.ShapeDtypeStruct((batch, SUBLANES, LANES), F32)]
    state_specs = [pl.BlockSpec((nb, dm.ch, HEAD_DIM, LANES), smap4),
                   pl.BlockSpec((nb, SUBLANES, LANES), smap)]
    if prompt:
        out_shape = [jax.ShapeDtypeStruct((batch, seq, d_model), F32),
                     jax.ShapeDtypeStruct((batch, win, dm.kvw), F32),
                     jax.ShapeDtypeStruct((batch, win, dm.kvw), F32)] + state_shapes
        out_specs = [pl.BlockSpec((nb, t, d_model), bmap),
                     pl.BlockSpec((1, win, dm.kvw), smap),
                     pl.BlockSpec((1, win, dm.kvw), smap)] + state_specs
    else:
        ck, cv, m0, mm0 = caches
        in_specs += [pl.BlockSpec((nb, win, dm.kvw), bmap), pl.BlockSpec((nb, win, dm.kvw), bmap),
                     pl.BlockSpec((nb, dm.ch, HEAD_DIM, LANES), smap4),
                     pl.BlockSpec((nb, SUBLANES, LANES), smap)]
        args += [ck, cv, m0, mm0]
        out_shape = [jax.ShapeDtypeStruct((batch, seq, d_model), F32),
                     jax.ShapeDtypeStruct((batch, seq, dm.aw), F32),
                     jax.ShapeDtypeStruct((batch, seq, dm.kvw), F32),
                     jax.ShapeDtypeStruct((batch, seq, dm.kvw), F32)] + state_shapes
        out_specs = [pl.BlockSpec((nb, t, d_model), bmap),
                     pl.BlockSpec((nb, t, dm.aw), bmap),
                     pl.BlockSpec((nb, t, dm.kvw), bmap),
                     pl.BlockSpec((nb, t, dm.kvw), bmap)] + state_specs
    scratch = [pltpu.VMEM((nb, dm.kv, win + t, HEAD_DIM), BF16),
               pltpu.VMEM((nb, dm.kv, win + t, HEAD_DIM), BF16),
               pltpu.VMEM((nb * t, d_model), F32)]
    return pl.pallas_call(
        functools.partial(_mixer_kernel, dm, prompt, nb, t, mchunk),
        grid=grid, in_specs=in_specs, out_specs=out_specs, out_shape=out_shape,
        scratch_shapes=scratch,
        compiler_params=pltpu.CompilerParams(dimension_semantics=sem,
                                             vmem_limit_bytes=VMEM_LIMIT_BYTES),
    )(*args)


def _ffn_kernel(prompt, final, nb, t, d_ff, *refs):
    if prompt:
        (x_ref, g_ref, wup_ref, cw_ref, cb_ref, wdn_ref, fin_ref,
         y_ref, cst_ref, ubuf, gbuf) = refs
    else:
        (x_ref, g_ref, wup_ref, cw_ref, cb_ref, wdn_ref, fin_ref, cc_ref,
         y_ref, cst_ref, ubuf, gbuf) = refs
    d_model = x_ref.shape[-1]
    rows = nb * t
    seg = SUBLANES + t
    hist = CONV_W - 1

    x = x_ref[...].reshape(rows, d_model)
    xn = _rmsnorm(x, g_ref[...]).astype(BF16)

    if prompt:
        @pl.when(pl.program_id(1) == 0)
        def _():
            ubuf[0:SUBLANES, :] = jnp.zeros((SUBLANES, 2 * d_ff), F32)
    else:
        for bi in range(nb):
            ubuf[bi * seg + SUBLANES - hist:bi * seg + SUBLANES, :] = cc_ref[bi]

    for j in range(d_ff // FFN_COLS):
        conv = []
        for half in range(2):
            c0 = half * d_ff + j * FFN_COLS
            cs = slice(c0, c0 + FFN_COLS)
            up = _dot(xn, wup_ref[:, cs])
            w0, w1, w2 = cw_ref[0:1, cs], cw_ref[1:2, cs], cw_ref[2:3, cs]
            parts = []
            for bi in range(nb):
                base = bi * seg + SUBLANES
                ub = up[bi * t:(bi + 1) * t]
                ubuf[base:base + t, cs] = ub
                parts.append(cb_ref[:, cs] + ubuf[base - 2:base - 2 + t, cs] * w0
                             + ubuf[base - 1:base - 1 + t, cs] * w1 + ub * w2)
                cst_ref[bi, :, cs] = ub[t - hist:t]
            conv.append(parts[0] if nb == 1 else jnp.concatenate(parts, axis=0))
        gbuf[:, j * FFN_COLS:(j + 1) * FFN_COLS] = (_gelu(conv[0]) * conv[1]).astype(BF16)

    if prompt:
        ubuf[0:SUBLANES, :] = ubuf[t:t + SUBLANES, :]
    y = x + _dot(gbuf[...], wdn_ref[...])
    if final:
        y = _rmsnorm(y, fin_ref[...])
    y_ref[...] = y.reshape(nb, t, d_model)


def _ffn_call(prompt, final, x, lw, final_norm, cache_conv=None):
    batch, seq, d_model = x.shape
    d_ff = lw["w_down"].shape[0]
    if prompt:
        nb, t = 1, min(PROMPT_TILE, seq)
        grid = (batch, seq // t)
        bmap = lambda b, s: (b, s, 0)
        smap = lambda b, s: (b, 0, 0)
        sem = ("arbitrary", "arbitrary")
    else:
        nb, t = min(SAMPLE_ROWS, batch), seq
        grid = (batch // nb,)
        bmap = lambda b: (b, 0, 0)
        smap = bmap
        sem = ("arbitrary",)
    assert seq % t == 0 and t % SUBLANES == 0 and batch % nb == 0 and d_ff % FFN_COLS == 0
    hist = CONV_W - 1
    in_specs = [
        pl.BlockSpec((nb, t, d_model), bmap),
        _const_spec((1, d_model)),
        _const_spec((d_model, 2 * d_ff)),
        _const_spec((CONV_W, 2 * d_ff)),
        _const_spec((1, 2 * d_ff)),
        _const_spec((d_ff, d_model)),
        _const_spec((1, d_model)),
    ]
    args = [x, lw["ffn_norm"], lw["w_up"], lw["conv_w"], lw["conv_b"], lw["w_down"], final_norm]
    if not prompt:
        in_specs.append(pl.BlockSpec((nb, hist, 2 * d_ff), bmap))
        args.append(cache_conv)
    out_shape = [jax.ShapeDtypeStruct((batch, seq, d_model), F32),
                 jax.ShapeDtypeStruct((batch, hist, 2 * d_ff), F32)]
    out_specs = [pl.BlockSpec((nb, t, d_model), bmap),
                 pl.BlockSpec((nb, hist, 2 * d_ff), smap)]
    scratch = [pltpu.VMEM((nb * (SUBLANES + t), 2 * d_ff), F32),
               pltpu.VMEM((nb * t, d_ff), BF16)]
    return pl.pallas_call(
        functools.partial(_ffn_kernel, prompt, final, nb, t, d_ff),
        grid=grid, in_specs=in_specs, out_specs=out_specs, out_shape=out_shape,
        scratch_shapes=scratch,
        compiler_params=pltpu.CompilerParams(dimension_semantics=sem,
                                             vmem_limit_bytes=VMEM_LIMIT_BYTES),
    )(*args)


def _rope_tables(pos):
    half = ROT_DIM // 2
    inv_freq = ROPE_THETA ** (-jnp.arange(half, dtype=F32) / half)
    ang = pos.astype(F32)[:, None] * inv_freq[None, :]
    cos, sin = jnp.cos(ang), jnp.sin(ang)
    ones = jnp.ones((pos.shape[0], HEAD_DIM - ROT_DIM), F32)
    zeros = jnp.zeros((pos.shape[0], HEAD_DIM - ROT_DIM), F32)
    zh = jnp.zeros_like(sin)
    cos_t = jnp.concatenate([cos, cos, ones], axis=1)
    sin_a = jnp.concatenate([-sin, zh, zeros], axis=1)
    sin_b = jnp.concatenate([zh, sin, zeros], axis=1)
    rep = LANES // HEAD_DIM
    return tuple(jnp.tile(a, (1, rep)) for a in (cos_t, sin_a, sin_b))


def _layer_weights(dm, l, attn_norm, w_in, gmlp_norm, gmlp_w, gmlp_b, attn_sinks, mlstm_gate_b,
                   w_out, ffn_norm, w_up, conv_w, conv_b, w_down):
    pad = dm.in_pad - dm.in_width
    return {
        "attn_norm": attn_norm[l][None, :],
        "w_in": jnp.pad(w_in[l], ((0, 0), (0, pad))).astype(BF16),
        "gate_b": jnp.pad(mlstm_gate_b[l], (0, LANES - 2 * dm.ch))[None, :],
        "gmlp_norm": gmlp_norm[l][None, :],
        "gmlp_w": gmlp_w[l],
        "gmlp_bx": jnp.repeat(gmlp_b[l].T, HEAD_DIM, axis=1),
        "sinks": attn_sinks[l],
        "w_out": w_out[l].astype(BF16),
        "ffn_norm": ffn_norm[l][None, :],
        "w_up": w_up[l].astype(BF16),
        "conv_w": conv_w[l],
        "conv_b": conv_b[l][None, :],
        "w_down": w_down[l].astype(BF16),
    }


def _pack_state(dm, state_c, state_n, state_m):
    b = state_c.shape[0]
    ct = jnp.swapaxes(state_c, -1, -2)
    packed = jnp.concatenate(
        [ct, state_n[..., None], jnp.zeros((b, dm.ch, HEAD_DIM, LANES - HEAD_DIM - 1), F32)], axis=-1)
    mm = jnp.pad(state_m, ((0, 0), (0, SUBLANES - dm.ch)))
    return packed, jnp.broadcast_to(mm[:, :, None], (b, SUBLANES, LANES))


def _unpack_state(dm, packed, mm):
    c = jnp.swapaxes(packed[..., :HEAD_DIM], -1, -2)
    return c, packed[..., HEAD_DIM], mm[:, :dm.ch, 0]


def kernel(x_prompt, x_sample, cache_k, cache_v, state_C, state_n, state_m, cache_conv, attn_norm, w_in, gmlp_norm, gmlp_w, gmlp_b, attn_sinks, mlstm_gate_b, w_out, ffn_norm, w_up, conv_w, conv_b, w_down, final_norm):
    depth = w_in.shape[0]
    d_model = x_prompt.shape[-1]
    kv_heads = cache_k.shape[3]
    dm = _Dims(d_model, kv_heads)
    assert w_in.shape[-1] == dm.in_width
    bp, sp = x_prompt.shape[:2]
    bs, ts = x_sample.shape[:2]

    tab_p = _rope_tables(jnp.arange(sp, dtype=jnp.int32))
    tab_s = _rope_tables(PAST_LEN + jnp.arange(ts, dtype=jnp.int32))
    fin = final_norm[None, :]

    xp, xs = x_prompt, x_sample
    p_st, s_st = [], []
    for l in range(depth):
        lw = _layer_weights(dm, l, attn_norm, w_in, gmlp_norm, gmlp_w, gmlp_b, attn_sinks,
                            mlstm_gate_b, w_out, ffn_norm, w_up, conv_w, conv_b, w_down)
        last = l == depth - 1
        x1, k_last, v_last, mst, mm = _mixer_call(dm, True, xp, tab_p, lw)
        xp, conv_p = _ffn_call(True, last, x1, lw, fin)
        c_p, n_p, m_p = _unpack_state(dm, mst, mm)
        p_st.append((k_last.reshape(bp, -1, kv_heads, HEAD_DIM), v_last.reshape(bp, -1, kv_heads, HEAD_DIM),
                     c_p, n_p, m_p, conv_p))
        m0, mm0 = _pack_state(dm, state_C[l], state_n[l], state_m[l])
        caches = (cache_k[l].reshape(bs, -1, dm.kvw), cache_v[l].reshape(bs, -1, dm.kvw), m0, mm0)
        x1, a_v, k_new, v_new, mst, mm = _mixer_call(dm, False, xs, tab_s, lw, caches)
        xs, conv_s = _ffn_call(False, last, x1, lw, fin, cache_conv[l])
        c_s, n_s, m_s = _unpack_state(dm, mst, mm)
        s_st.append((a_v, k_new.reshape(bs, ts, kv_heads, HEAD_DIM), v_new.reshape(bs, ts, kv_heads, HEAD_DIM),
                     c_s, n_s, m_s, conv_s))

    def stk(states, i):
        return jnp.stack([s[i] for s in states], axis=0)

    return (xp, xs,
            stk(p_st, 0), stk(p_st, 1), stk(p_st, 2), stk(p_st, 3), stk(p_st, 4), stk(p_st, 5),
            stk(s_st, 0), stk(s_st, 1), stk(s_st, 2), stk(s_st, 3), stk(s_st, 4), stk(s_st, 5), stk(s_st, 6))
```

```python
import functools

import numpy as np
import jax
import jax.numpy as jnp
from jax import lax
from jax.experimental import pallas as pl
from jax.experimental.pallas import tpu as pltpu

HEAD_DIM = 64
CHUNK = 64
A_BLOCK = 128
ROT_DIM = 16
ROPE_THETA = 500000.0
ATTN_SCALE = HEAD_DIM ** -0.5
RMS_EPS = 1e-6
CONV_W = 3
PAST_LEN = 1024

LANES = 128
SUBLANES = 8
BF16_ROWS = 16
VMEM_LIMIT_BYTES = 56 * 1024 * 1024

PROMPT_TILE = 256
MLSTM_CHUNK = 128
SAMPLE_ROWS = 4
FFN_COLS = 256

F32 = jnp.float32
BF16 = jnp.bfloat16


def _dot(a, b):
    return jnp.dot(a, b, preferred_element_type=F32)


def _dot_nt(a, b):
    return lax.dot_general(a, b, (((1,), (1,)), ((), ())), preferred_element_type=F32)


def _rmsnorm(x, g):
    ms = jnp.mean(x * x, axis=-1, keepdims=True)
    return x * lax.rsqrt(ms + RMS_EPS) * g


def _gelu(x):
    return 0.5 * x * (1.0 + lax.erf(x * np.float32(np.sqrt(0.5))))


def _log_sigmoid(x):
    return jnp.minimum(x, 0.0) - jnp.log1p(jnp.exp(-jnp.abs(x)))


class _Dims:
    def __init__(self, d_model, kv_heads):
        self.d = d_model
        self.aw = d_model // 4
        self.ah = self.aw // HEAD_DIM
        self.bw = d_model // 2
        self.bh = self.bw // HEAD_DIM
        self.kv = kv_heads
        self.grp = self.bh // kv_heads
        self.kvw = kv_heads * HEAD_DIM
        self.cw = d_model // 4
        self.ch = self.cw // HEAD_DIM
        self.off_a_u = 0
        self.off_a_v = self.off_a_u + self.aw
        self.off_b_q = self.off_a_v + self.aw
        self.off_b_k = self.off_b_q + self.bw
        self.off_b_v = self.off_b_k + self.kvw
        self.off_c_q = self.off_b_v + self.kvw
        self.off_c_k = self.off_c_q + self.cw
        self.off_c_v = self.off_c_k + self.cw
        self.off_c_o = self.off_c_v + self.cw
        self.off_c_g = self.off_c_o + self.cw
        self.in_width = self.off_c_g + 2 * self.ch
        self.tm_a_u = 0
        self.tm_a_v = self.tm_a_u + self.aw
        self.tm_b_q = self.tm_a_v + self.aw
        self.tm_b_k = self.tm_b_q + self.bw
        self.tm_b_v = self.tm_b_k + self.kvw
        self.tm_c_k = self.tm_b_v + self.kvw
        self.tm_c_o = self.tm_c_k + self.cw
        self.tm_width = self.tm_c_o + self.cw
        self.fm_c_q = 0
        self.fm_c_v = self.fm_c_q + self.cw
        self.fm_g = self.fm_c_v + self.cw
        self.fm_rows = self.fm_g + 2 * SUBLANES
        assert self.kvw == LANES and 2 * self.ch == SUBLANES and self.grp % 2 == 0
        assert self.fm_rows % BF16_ROWS == 0 and self.ch % 2 == 0


def _scan_lanes(x, length, op, fill):
    outs = []
    for j in range(x.shape[1] // LANES):
        xj = x[:, j * LANES:(j + 1) * LANES]
        pos = lax.broadcasted_iota(jnp.int32, xj.shape, 1) & (length - 1)
        s = 1
        while s < length:
            xj = op(xj, jnp.where(pos >= s, pltpu.roll(xj, s, 1), fill))
            s *= 2
        outs.append(xj)
    return outs[0] if len(outs) == 1 else jnp.concatenate(outs, axis=1)


def _rope(xt, cos_t, sin_a, sin_b, nb, t):
    up = pltpu.roll(xt, LANES - ROT_DIM // 2, 1)
    dn = pltpu.roll(xt, ROT_DIM // 2, 1)
    shp = (nb, t, LANES)
    out = (xt.reshape(shp) * cos_t[None] + up.reshape(shp) * sin_a[None]
           + dn.reshape(shp) * sin_b[None])
    return out.reshape(nb * t, LANES)


def _last_lane(row, length):
    lane = lax.broadcasted_iota(jnp.int32, row.shape, 1)
    return jnp.max(jnp.where(lane == length - 1, row, -jnp.inf), axis=-1, keepdims=True)


def _mlstm_chunk(k_pair, odd, q_t, v_t, b_row, r_row, cm_row, r_full, state, m_prev, causal_t):
    L = q_t.shape[1]
    zeros = jnp.zeros_like(q_t)
    q_ext = jnp.concatenate([zeros, q_t] if odd else [q_t, zeros], axis=0).astype(BF16)
    g_row = jnp.maximum(m_prev[:, :L], cm_row)
    m_t = g_row + b_row
    p_t = jnp.where(causal_t, jnp.exp(r_full[:, :L] - g_row), 0.0)
    s_t = (_dot(k_pair, q_ext) * p_t).astype(BF16)
    v_ext = jnp.concatenate([v_t, jnp.ones_like(v_t)], axis=0)
    intra = _dot(v_ext.astype(BF16), s_t)
    inter = _dot(state.astype(BF16), q_ext)
    comb = jnp.exp(m_prev[:, :L] - g_row) * inter + intra
    den = comb[HEAD_DIM:HEAD_DIM + 1, :]
    h_t = comb[:HEAD_DIM, :] / jnp.maximum(jnp.abs(den), jnp.exp(-m_t))
    m_new = _last_lane(m_t, L)
    b_last = _last_lane(b_row, L)
    decay = jnp.exp(b_last + m_prev - m_new)
    w_row = jnp.exp(r_row + (b_last - m_new))
    upd = _dot((v_ext * w_row).astype(BF16), k_pair)
    lane = lax.broadcasted_iota(jnp.int32, upd.shape, 1)
    own = (lane >= HEAD_DIM) if odd else (lane < HEAD_DIM)
    state_new = decay * state + jnp.where(own, upd, 0.0)
    return h_t, state_new, jnp.broadcast_to(m_new, (1, LANES))


def _mixer_kernel(dm, prompt, nb, t, mchunk, *refs):
    if prompt:
        (x_ref, cos_ref, sa_ref, sb_ref, ng_ref, wtm_ref, wfm_ref, gb_ref, gg_ref, gw_ref, gbx_ref,
         sinks_ref, wout_ref,
         x1_ref, klast_ref, vlast_ref, mst_ref, mm_ref,
         kbuf, vbuf, mixed_ref, ht_ref) = refs
    else:
        (x_ref, cos_ref, sa_ref, sb_ref, ng_ref, wtm_ref, wfm_ref, gb_ref, gg_ref, gw_ref, gbx_ref,
         sinks_ref, wout_ref, ck_ref, cv_ref, m0_ref, mm0_ref,
         x1_ref, av_ref, knew_ref, vnew_ref, mst_ref, mm_ref,
         kbuf, vbuf, mixed_ref, ht_ref) = refs
    d_model = dm.d
    rows = nb * t
    win = 2 * CHUNK
    step = pl.program_id(1) if prompt else None

    x = x_ref[...].reshape(rows, d_model)
    xn = _rmsnorm(x, ng_ref[...]).astype(BF16)

    def sec(off, width):
        return _dot(xn, wtm_ref[:, off:off + width])

    cos_t, sin_a, sin_b = cos_ref[...], sa_ref[...], sb_ref[...]

    a_u = _gelu(sec(dm.tm_a_u, dm.aw))
    a_v = _rmsnorm(_gelu(sec(dm.tm_a_v, dm.aw)), gg_ref[...])
    zq = sec(dm.tm_b_q, dm.bw)
    q_tiles = [_rope(zq[:, j * LANES:(j + 1) * LANES], cos_t, sin_a, sin_b, nb, t) * ATTN_SCALE
               for j in range(dm.bw // LANES)]
    k_rot = _rope(sec(dm.tm_b_k, dm.kvw), cos_t, sin_a, sin_b, nb, t)
    v_new = sec(dm.tm_b_v, dm.kvw)
    c_k = (sec(dm.tm_c_k, dm.cw) * ATTN_SCALE).astype(BF16)
    c_o = jax.nn.sigmoid(sec(dm.tm_c_o, dm.cw))
    fm = _dot_nt(wfm_ref[...], xn)
    g_if = fm[dm.fm_g:dm.fm_g + SUBLANES, :] + gb_ref[0:SUBLANES, 0:1]
    g_fi = fm[dm.fm_g + SUBLANES:dm.fm_g + 2 * SUBLANES, :] + gb_ref[SUBLANES:2 * SUBLANES, 0:1]
    cum_b = _scan_lanes(_log_sigmoid(g_fi), mchunk, jnp.add, 0.0)
    r_all = g_if - cum_b
    cm_all = _scan_lanes(r_all, mchunk, jnp.maximum, -jnp.inf)

    lane_kv = lax.broadcasted_iota(jnp.int32, (rows, LANES), 1)
    k_sw, v_sw = pltpu.roll(k_rot, HEAD_DIM, 1), pltpu.roll(v_new, HEAD_DIM, 1)
    ones_t = jnp.ones((t, LANES), BF16)

    def dup(a, a_sw, g):
        lo = lane_kv < HEAD_DIM
        return (jnp.where(lo, a, a_sw) if g == 0 else jnp.where(lo, a_sw, a)).astype(BF16)

    if prompt:
        @pl.when(step == 0)
        def _():
            kbuf[:, :, 0:win, :] = jnp.zeros((nb, dm.kv, win, LANES), BF16)
            vbuf[:, :, 0:win, :] = jnp.zeros((nb, dm.kv, win, 2 * LANES), BF16)
            mst_ref[...] = jnp.zeros(mst_ref.shape, F32)
            mm_ref[...] = jnp.zeros(mm_ref.shape, F32)
    else:
        lane_c = lax.broadcasted_iota(jnp.int32, (win, LANES), 1) < HEAD_DIM
        ones_w = jnp.ones((win, LANES), BF16)
        for bi in range(nb):
            ck, cv = ck_ref[bi], cv_ref[bi]
            ck_sw, cv_sw = pltpu.roll(ck, HEAD_DIM, 1), pltpu.roll(cv, HEAD_DIM, 1)
            for g in range(dm.kv):
                ka, kb = (ck, ck_sw) if g == 0 else (ck_sw, ck)
                va, vb = (cv, cv_sw) if g == 0 else (cv_sw, cv)
                kbuf[bi, g, 0:win, :] = jnp.where(lane_c, ka, kb).astype(BF16)
                vbuf[bi, g, 0:win, 0:LANES] = jnp.where(lane_c, va, vb).astype(BF16)
                vbuf[bi, g, 0:win, LANES:2 * LANES] = ones_w
    for g in range(dm.kv):
        kd, vd = dup(k_rot, k_sw, g), dup(v_new, v_sw, g)
        for bi in range(nb):
            kbuf[bi, g, win:win + t, :] = kd[bi * t:(bi + 1) * t]
            vbuf[bi, g, win:win + t, 0:LANES] = vd[bi * t:(bi + 1) * t]
            vbuf[bi, g, win:win + t, LANES:2 * LANES] = ones_t

    blk = min(A_BLOCK, t)
    ri = lax.broadcasted_iota(jnp.int32, (A_BLOCK, A_BLOCK), 0)
    ci = lax.broadcasted_iota(jnp.int32, (A_BLOCK, A_BLOCK), 1)
    block_causal = (ci // CHUNK) <= (ri // CHUNK)
    w_sp = [jnp.where(block_causal, gw_ref[h], 0.0)[:blk, :blk].astype(BF16) for h in range(dm.ah)]
    lane_a = lax.broadcasted_iota(jnp.int32, (blk, dm.aw), 1) // HEAD_DIM
    gbx = gbx_ref[0:blk, :]
    for r0 in range(0, rows, blk):
        vb = a_v[r0:r0 + blk].astype(BF16)
        mixed = _dot(w_sp[dm.ah - 1], vb)
        for h in range(dm.ah - 2, -1, -1):
            mixed = jnp.where(lane_a == h, _dot(w_sp[h], vb), mixed)
        mixed_ref[r0:r0 + blk, 0:dm.aw] = a_u[r0:r0 + blk] * (mixed + gbx)

    lane_q = lax.broadcasted_iota(jnp.int32, (CHUNK, LANES), 1) < HEAD_DIM
    key_lane = lax.broadcasted_iota(jnp.int32, (1, win + CHUNK), 1)
    for bi in range(nb):
        for c in range(t // CHUNK):
            r0 = bi * t + c * CHUNK
            key_ok = None
            if prompt and c < win // CHUNK:
                key_ok = (step * t + c * CHUNK - win + key_lane) >= 0
            for g in range(dm.kv):
                parts = []
                for j in range(dm.grp):
                    tile = q_tiles[(g * dm.grp + j) // 2][r0:r0 + CHUNK]
                    parts.append(jnp.where(lane_q if j % 2 == 0 else ~lane_q, tile, 0.0))
                q4 = jnp.concatenate(parts, axis=0).astype(BF16)
                ks = slice(c * CHUNK, c * CHUNK + win + CHUNK)
                s = _dot_nt(q4, kbuf[bi, g, ks, :])
                if key_ok is not None:
                    s = jnp.where(key_ok, s, -jnp.inf)
                p_parts, mx_parts = [], []
                for j in range(dm.grp):
                    sj = s[j * CHUNK:(j + 1) * CHUNK]
                    mxj = jnp.maximum(jnp.max(sj, axis=-1, keepdims=True),
                                      sinks_ref[g * dm.grp + j])
                    p_parts.append(jnp.exp(sj - mxj))
                    mx_parts.append(mxj)
                p = jnp.concatenate(p_parts, axis=0).astype(BF16)
                o2 = _dot(p, vbuf[bi, g, ks, :])
                outs = []
                for j in range(dm.grp):
                    rs = slice(j * CHUNK, (j + 1) * CHUNK)
                    den = o2[rs, LANES:2 * LANES] + jnp.exp(sinks_ref[g * dm.grp + j] - mx_parts[j])
                    outs.append(o2[rs, 0:LANES] / den)
                for i in range(dm.grp // 2):
                    col = dm.aw + (g * dm.grp + 2 * i) * HEAD_DIM
                    mixed_ref[r0:r0 + CHUNK, col:col + LANES] = jnp.where(
                        lane_q, outs[2 * i], outs[2 * i + 1])

    si = lax.broadcasted_iota(jnp.int32, (mchunk, mchunk), 0)
    ti = lax.broadcasted_iota(jnp.int32, (mchunk, mchunk), 1)
    causal_t = si <= ti
    for h in range(dm.ch):
        tile = slice((h // 2) * LANES, (h // 2 + 1) * LANES)
        q_h = fm[dm.fm_c_q + h * HEAD_DIM:dm.fm_c_q + (h + 1) * HEAD_DIM, :]
        v_h = fm[dm.fm_c_v + h * HEAD_DIM:dm.fm_c_v + (h + 1) * HEAD_DIM, :]
        r_full = jnp.broadcast_to(r_all[h:h + 1, :], (LANES, rows)).T
        for bi in range(nb):
            if prompt:
                state, m_prev = mst_ref[bi, h], mm_ref[bi, h:h + 1, :]
            else:
                state, m_prev = m0_ref[bi, h], mm0_ref[bi, h:h + 1, :]
            for c in range(t // mchunk):
                r0 = bi * t + c * mchunk
                rs = slice(r0, r0 + mchunk)
                h_t, state, m_prev = _mlstm_chunk(
                    c_k[rs, tile], h % 2 == 1, q_h[:, rs], v_h[:, rs],
                    cum_b[h:h + 1, rs], r_all[h:h + 1, rs], cm_all[h:h + 1, rs], r_full[rs],
                    state, m_prev, causal_t)
                ht_ref[h * HEAD_DIM:(h + 1) * HEAD_DIM, rs] = h_t
            mst_ref[bi, h] = state
            mm_ref[bi, h:h + 1, :] = m_prev
    col = dm.aw + dm.bw
    mixed_ref[:, col:col + dm.cw] = c_o * ht_ref[...].T

    x1 = x + _dot(mixed_ref[...].astype(BF16), wout_ref[...])
    x1_ref[...] = x1.reshape(nb, t, d_model)

    if prompt:
        @pl.when(step == pl.num_programs(1) - 1)
        def _():
            klast_ref[0] = k_rot[t - win:t]
            vlast_ref[0] = v_new[t - win:t]

        for g in range(dm.kv):
            kbuf[0, g, 0:win, :] = kbuf[0, g, t:t + win, :]
            vbuf[0, g, 0:win, :] = vbuf[0, g, t:t + win, :]
    else:
        av_ref[...] = a_v.reshape(nb, t, dm.aw)
        knew_ref[...] = k_rot.reshape(nb, t, dm.kvw)
        vnew_ref[...] = v_new.reshape(nb, t, dm.kvw)


def _const_spec(shape):
    zeros = (0,) * len(shape)
    return pl.BlockSpec(shape, lambda *_: zeros, pipeline_mode=pl.Buffered(1))


def _mixer_call(dm, prompt, x, tables, lw, caches=None):
    batch, seq, d_model = x.shape
    if prompt:
        nb, t, mchunk = 1, min(PROMPT_TILE, seq), min(MLSTM_CHUNK, seq)
        grid = (batch, seq // t)
        bmap = lambda b, s: (b, s, 0)
        smap = lambda b, s: (b, 0, 0)
        smap4 = lambda b, s: (b, 0, 0, 0)
        tmap = lambda b, s: (s, 0)
        sem = ("arbitrary", "arbitrary")
    else:
        nb, t, mchunk = min(SAMPLE_ROWS, batch), seq, seq
        grid = (batch // nb,)
        bmap = lambda b: (b, 0, 0)
        smap = bmap
        smap4 = lambda b: (b, 0, 0, 0)
        tmap = lambda b: (0, 0)
        sem = ("arbitrary",)
    assert seq % t == 0 and t % mchunk == 0 and t % CHUNK == 0 and (nb * t) % LANES == 0
    assert batch % nb == 0 and mchunk & (mchunk - 1) == 0 and mchunk <= LANES
    win = 2 * CHUNK

    in_specs = [
        pl.BlockSpec((nb, t, d_model), bmap),
        pl.BlockSpec((t, LANES), tmap), pl.BlockSpec((t, LANES), tmap), pl.BlockSpec((t, LANES), tmap),
        _const_spec((1, d_model)),
        _const_spec((d_model, dm.tm_width)),
        _const_spec((dm.fm_rows, d_model)),
        _const_spec((2 * SUBLANES, LANES)),
        _const_spec((1, dm.aw)),
        _const_spec((dm.ah, A_BLOCK, A_BLOCK)),
        _const_spec((A_BLOCK, dm.aw)),
        pl.BlockSpec(memory_space=pltpu.SMEM),
        _const_spec((d_model, d_model)),
    ]
    args = [x, *tables, lw["attn_norm"], lw["w_tm"], lw["w_fm"], lw["gate_b"], lw["gmlp_norm"],
            lw["gmlp_w"], lw["gmlp_bx"], lw["sinks"], lw["w_out"]]
    state_shapes = [jax.ShapeDtypeStruct((batch, dm.ch, LANES, LANES), F32),
                    jax.ShapeDtypeStruct((batch, SUBLANES, LANES), F32)]
    state_specs = [pl.BlockSpec((nb, dm.ch, LANES, LANES), smap4),
                   pl.BlockSpec((nb, SUBLANES, LANES), smap)]
    if prompt:
        out_shape = [jax.ShapeDtypeStruct((batch, seq, d_model), F32),
                     jax.ShapeDtypeStruct((batch, win, dm.kvw), F32),
                     jax.ShapeDtypeStruct((batch, win, dm.kvw), F32)] + state_shapes
        out_specs = [pl.BlockSpec((nb, t, d_model), bmap),
                     pl.BlockSpec((1, win, dm.kvw), smap),
                     pl.BlockSpec((1, win, dm.kvw), smap)] + state_specs
    else:
        ck, cv, m0, mm0 = caches
        in_specs += [pl.BlockSpec((nb, win, dm.kvw), bmap), pl.BlockSpec((nb, win, dm.kvw), bmap),
                     pl.BlockSpec((nb, dm.ch, LANES, LANES), smap4),
                     pl.BlockSpec((nb, SUBLANES, LANES), smap)]
        args += [ck, cv, m0, mm0]
        out_shape = [jax.ShapeDtypeStruct((batch, seq, d_model), F32),
                     jax.ShapeDtypeStruct((batch, seq, dm.aw), F32),
                     jax.ShapeDtypeStruct((batch, seq, dm.kvw), F32),
                     jax.ShapeDtypeStruct((batch, seq, dm.kvw), F32)] + state_shapes
        out_specs = [pl.BlockSpec((nb, t, d_model), bmap),
                     pl.BlockSpec((nb, t, dm.aw), bmap),
                     pl.BlockSpec((nb, t, dm.kvw), bmap),
                     pl.BlockSpec((nb, t, dm.kvw), bmap)] + state_specs
    scratch = [pltpu.VMEM((nb, dm.kv, win + t, LANES), BF16),
               pltpu.VMEM((nb, dm.kv, win + t, 2 * LANES), BF16),
               pltpu.VMEM((nb * t, d_model), F32),
               pltpu.VMEM((dm.cw, nb * t), F32)]
    return pl.pallas_call(
        functools.partial(_mixer_kernel, dm, prompt, nb, t, mchunk),
        grid=grid, in_specs=in_specs, out_specs=out_specs, out_shape=out_shape,
        scratch_shapes=scratch,
        compiler_params=pltpu.CompilerParams(dimension_semantics=sem,
                                             vmem_limit_bytes=VMEM_LIMIT_BYTES),
    )(*args)


def _ffn_kernel(prompt, final, nb, t, d_ff, *refs):
    if prompt:
        (x_ref, g_ref, wup_ref, cw_ref, cb_ref, wdn_ref, fin_ref,
         y_ref, cst_ref, ubuf, gbuf) = refs
    else:
        (x_ref, g_ref, wup_ref, cw_ref, cb_ref, wdn_ref, fin_ref, cc_ref,
         y_ref, cst_ref, ubuf, gbuf) = refs
    d_model = x_ref.shape[-1]
    rows = nb * t
    seg = SUBLANES + t
    hist = CONV_W - 1

    x = x_ref[...].reshape(rows, d_model)
    xn = _rmsnorm(x, g_ref[...]).astype(BF16)

    if prompt:
        @pl.when(pl.program_id(1) == 0)
        def _():
            ubuf[0:SUBLANES, :] = jnp.zeros((SUBLANES, 2 * d_ff), F32)
    else:
        for bi in range(nb):
            ubuf[bi * seg + SUBLANES - hist:bi * seg + SUBLANES, :] = cc_ref[bi]

    for j in range(d_ff // FFN_COLS):
        conv = []
        for half in range(2):
            c0 = half * d_ff + j * FFN_COLS
            cs = slice(c0, c0 + FFN_COLS)
            up = _dot(xn, wup_ref[:, cs])
            w0, w1, w2 = cw_ref[0:1, cs], cw_ref[1:2, cs], cw_ref[2:3, cs]
            parts = []
            for bi in range(nb):
                base = bi * seg + SUBLANES
                ub = up[bi * t:(bi + 1) * t]
                ubuf[base:base + t, cs] = ub
                parts.append(cb_ref[:, cs] + ubuf[base - 2:base - 2 + t, cs] * w0
                             + ubuf[base - 1:base - 1 + t, cs] * w1 + ub * w2)
                cst_ref[bi, :, cs] = ub[t - hist:t]
            conv.append(parts[0] if nb == 1 else jnp.concatenate(parts, axis=0))
        gbuf[:, j * FFN_COLS:(j + 1) * FFN_COLS] = (_gelu(conv[0]) * conv[1]).astype(BF16)

    if prompt:
        ubuf[0:SUBLANES, :] = ubuf[t:t + SUBLANES, :]
    y = x + _dot(gbuf[...], wdn_ref[...])
    if final:
        y = _rmsnorm(y, fin_ref[...])
    y_ref[...] = y.reshape(nb, t, d_model)


def _ffn_call(prompt, final, x, lw, final_norm, cache_conv=None):
    batch, seq, d_model = x.shape
    d_ff = lw["w_down"].shape[0]
    if prompt:
        nb, t = 1, min(PROMPT_TILE, seq)
        grid = (batch, seq // t)
        bmap = lambda b, s: (b, s, 0)
        smap = lambda b, s: (b, 0, 0)
        sem = ("arbitrary", "arbitrary")
    else:
        nb, t = min(SAMPLE_ROWS, batch), seq
        grid = (batch // nb,)
        bmap = lambda b: (b, 0, 0)
        smap = bmap
        sem = ("arbitrary",)
    assert seq % t == 0 and t % SUBLANES == 0 and batch % nb == 0 and d_ff % FFN_COLS == 0
    hist = CONV_W - 1
    in_specs = [
        pl.BlockSpec((nb, t, d_model), bmap),
        _const_spec((1, d_model)),
        _const_spec((d_model, 2 * d_ff)),
        _const_spec((CONV_W, 2 * d_ff)),
        _const_spec((1, 2 * d_ff)),
        _const_spec((d_ff, d_model)),
        _const_spec((1, d_model)),
    ]
    args = [x, lw["ffn_norm"], lw["w_up"], lw["conv_w"], lw["conv_b"], lw["w_down"], final_norm]
    if not prompt:
        in_specs.append(pl.BlockSpec((nb, hist, 2 * d_ff), bmap))
        args.append(cache_conv)
    out_shape = [jax.ShapeDtypeStruct((batch, seq, d_model), F32),
                 jax.ShapeDtypeStruct((batch, hist, 2 * d_ff), F32)]
    out_specs = [pl.BlockSpec((nb, t, d_model), bmap),
                 pl.BlockSpec((nb, hist, 2 * d_ff), smap)]
    scratch = [pltpu.VMEM((nb * (SUBLANES + t), 2 * d_ff), F32),
               pltpu.VMEM((nb * t, d_ff), BF16)]
    return pl.pallas_call(
        functools.partial(_ffn_kernel, prompt, final, nb, t, d_ff),
        grid=grid, in_specs=in_specs, out_specs=out_specs, out_shape=out_shape,
        scratch_shapes=scratch,
        compiler_params=pltpu.CompilerParams(dimension_semantics=sem,
                                             vmem_limit_bytes=VMEM_LIMIT_BYTES),
    )(*args)


def _rope_tables(pos):
    half = ROT_DIM // 2
    inv_freq = ROPE_THETA ** (-jnp.arange(half, dtype=F32) / half)
    ang = pos.astype(F32)[:, None] * inv_freq[None, :]
    cos, sin = jnp.cos(ang), jnp.sin(ang)
    ones = jnp.ones((pos.shape[0], HEAD_DIM - ROT_DIM), F32)
    zeros = jnp.zeros((pos.shape[0], HEAD_DIM - ROT_DIM), F32)
    zh = jnp.zeros_like(sin)
    cos_t = jnp.concatenate([cos, cos, ones], axis=1)
    sin_a = jnp.concatenate([-sin, zh, zeros], axis=1)
    sin_b = jnp.concatenate([zh, sin, zeros], axis=1)
    rep = LANES // HEAD_DIM
    return tuple(jnp.tile(a, (1, rep)) for a in (cos_t, sin_a, sin_b))


def _layer_weights(dm, l, attn_norm, w_in, gmlp_norm, gmlp_w, gmlp_b, attn_sinks, mlstm_gate_b,
                   w_out, ffn_norm, w_up, conv_w, conv_b, w_down):
    w = w_in[l]

    def cols(off, width):
        return w[:, off:off + width]

    w_tm = jnp.concatenate([cols(dm.off_a_u, dm.aw), cols(dm.off_a_v, dm.aw), cols(dm.off_b_q, dm.bw),
                            cols(dm.off_b_k, dm.kvw), cols(dm.off_b_v, dm.kvw), cols(dm.off_c_k, dm.cw),
                            cols(dm.off_c_o, dm.cw)], axis=1)
    g_i, g_f = cols(dm.off_c_g, dm.ch), cols(dm.off_c_g + dm.ch, dm.ch)
    w_fm = jnp.concatenate([cols(dm.off_c_q, dm.cw), cols(dm.off_c_v, dm.cw), g_i, g_f, g_f, g_i],
                           axis=1).T
    b_i, b_f = mlstm_gate_b[l][:dm.ch], mlstm_gate_b[l][dm.ch:]
    gate_b = jnp.concatenate([b_i, b_f, b_f, b_i])
    return {
        "attn_norm": attn_norm[l][None, :],
        "w_tm": w_tm.astype(BF16),
        "w_fm": w_fm.astype(BF16),
        "gate_b": jnp.broadcast_to(gate_b[:, None], (2 * SUBLANES, LANES)),
        "gmlp_norm": gmlp_norm[l][None, :],
        "gmlp_w": gmlp_w[l],
        "gmlp_bx": jnp.repeat(gmlp_b[l].T, HEAD_DIM, axis=1),
        "sinks": attn_sinks[l],
        "w_out": w_out[l].astype(BF16),
        "ffn_norm": ffn_norm[l][None, :],
        "w_up": w_up[l].astype(BF16),
        "conv_w": conv_w[l],
        "conv_b": conv_b[l][None, :],
        "w_down": w_down[l].astype(BF16),
    }


def _pack_state(dm, state_c, state_n, state_m):
    b = state_c.shape[0]
    cn = jnp.concatenate(
        [state_c, jnp.broadcast_to(state_n[:, :, None, :], (b, dm.ch, HEAD_DIM, HEAD_DIM))], axis=2)
    z = jnp.zeros_like(cn)
    odd = (jnp.arange(dm.ch) % 2 == 1)[None, :, None, None]
    packed = jnp.where(odd, jnp.concatenate([z, cn], axis=-1), jnp.concatenate([cn, z], axis=-1))
    mm = jnp.pad(state_m, ((0, 0), (0, SUBLANES - dm.ch)))
    return packed, jnp.broadcast_to(mm[:, :, None], (b, SUBLANES, LANES))


def _unpack_state(dm, packed, mm):
    odd = (jnp.arange(dm.ch) % 2 == 1)[None, :, None, None]
    cn = jnp.where(odd, packed[..., HEAD_DIM:], packed[..., :HEAD_DIM])
    return cn[:, :, :HEAD_DIM, :], cn[:, :, HEAD_DIM, :], mm[:, :dm.ch, 0]


def kernel(x_prompt, x_sample, cache_k, cache_v, state_C, state_n, state_m, cache_conv, attn_norm, w_in, gmlp_norm, gmlp_w, gmlp_b, attn_sinks, mlstm_gate_b, w_out, ffn_norm, w_up, conv_w, conv_b, w_down, final_norm):
    depth = w_in.shape[0]
    d_model = x_prompt.shape[-1]
    kv_heads = cache_k.shape[3]
    dm = _Dims(d_model, kv_heads)
    assert w_in.shape[-1] == dm.in_width
    bp, sp = x_prompt.shape[:2]
    bs, ts = x_sample.shape[:2]

    tab_p = _rope_tables(jnp.arange(sp, dtype=jnp.int32))
    tab_s = _rope_tables(PAST_LEN + jnp.arange(ts, dtype=jnp.int32))
    fin = final_norm[None, :]

    xp, xs = x_prompt, x_sample
    p_st, s_st = [], []
    for l in range(depth):
        lw = _layer_weights(dm, l, attn_norm, w_in, gmlp_norm, gmlp_w, gmlp_b, attn_sinks,
                            mlstm_gate_b, w_out, ffn_norm, w_up, conv_w, conv_b, w_down)
        last = l == depth - 1
        x1, k_last, v_last, mst, mm = _mixer_call(dm, True, xp, tab_p, lw)
        xp, conv_p = _ffn_call(True, last, x1, lw, fin)
        c_p, n_p, m_p = _unpack_state(dm, mst, mm)
        p_st.append((k_last.reshape(bp, -1, kv_heads, HEAD_DIM), v_last.reshape(bp, -1, kv_heads, HEAD_DIM),
                     c_p, n_p, m_p, conv_p))
        m0, mm0 = _pack_state(dm, state_C[l], state_n[l], state_m[l])
        caches = (cache_k[l].reshape(bs, -1, dm.kvw), cache_v[l].reshape(bs, -1, dm.kvw), m0, mm0)
        x1, a_v, k_new, v_new, mst, mm = _mixer_call(dm, False, xs, tab_s, lw, caches)
        xs, conv_s = _ffn_call(False, last, x1, lw, fin, cache_conv[l])
        c_s, n_s, m_s = _unpack_state(dm, mst, mm)
        s_st.append((a_v, k_new.reshape(bs, ts, kv_heads, HEAD_DIM), v_new.reshape(bs, ts, kv_heads, HEAD_DIM),
                     c_s, n_s, m_s, conv_s))

    def stk(states, i):
        return jnp.stack([s[i] for s in states], axis=0)

    return (xp, xs,
            stk(p_st, 0), stk(p_st, 1), stk(p_st, 2), stk(p_st, 3), stk(p_st, 4), stk(p_st, 5),
            stk(s_st, 0), stk(s_st, 1), stk(s_st, 2), stk(s_st, 3), stk(s_st, 4), stk(s_st, 5), stk(s_st, 6))
```

```python
import functools

import numpy as np
import jax
import jax.numpy as jnp
from jax import lax
from jax.experimental import pallas as pl
from jax.experimental.pallas import tpu as pltpu

HEAD_DIM = 64
CHUNK = 64
A_BLOCK = 128
ROT_DIM = 16
ROPE_THETA = 500000.0
ATTN_SCALE = HEAD_DIM ** -0.5
RMS_EPS = 1e-6
CONV_W = 3
PAST_LEN = 1024

LANES = 128
SUBLANES = 8
BF16_ROWS = 16
VMEM_LIMIT_BYTES = 56 * 1024 * 1024

PROMPT_TILE = 512
FFN_TILE = 512
FFN_DOWN_GROUP = 4
MLSTM_CHUNK = 128
SAMPLE_ROWS = 8
FFN_COLS = 256

F32 = jnp.float32
BF16 = jnp.bfloat16


def _dot(a, b):
    return jnp.dot(a, b, preferred_element_type=F32)


def _dot_nt(a, b):
    return lax.dot_general(a, b, (((1,), (1,)), ((), ())), preferred_element_type=F32)


def _rmsnorm(x, g):
    ms = jnp.mean(x * x, axis=-1, keepdims=True)
    return x * lax.rsqrt(ms + RMS_EPS) * g


def _gelu(x):
    return 0.5 * x * (1.0 + lax.erf(x * np.float32(np.sqrt(0.5))))


def _log_sigmoid(x):
    return jnp.minimum(x, 0.0) - jnp.log1p(jnp.exp(-jnp.abs(x)))


class _Dims:
    def __init__(self, d_model, kv_heads):
        self.d = d_model
        self.aw = d_model // 4
        self.ah = self.aw // HEAD_DIM
        self.bw = d_model // 2
        self.bh = self.bw // HEAD_DIM
        self.kv = kv_heads
        self.grp = self.bh // kv_heads
        self.kvw = kv_heads * HEAD_DIM
        self.cw = d_model // 4
        self.ch = self.cw // HEAD_DIM
        self.off_a_u = 0
        self.off_a_v = self.off_a_u + self.aw
        self.off_b_q = self.off_a_v + self.aw
        self.off_b_k = self.off_b_q + self.bw
        self.off_b_v = self.off_b_k + self.kvw
        self.off_c_q = self.off_b_v + self.kvw
        self.off_c_k = self.off_c_q + self.cw
        self.off_c_v = self.off_c_k + self.cw
        self.off_c_o = self.off_c_v + self.cw
        self.off_c_g = self.off_c_o + self.cw
        self.in_width = self.off_c_g + 2 * self.ch
        self.tm_a_u = 0
        self.tm_a_v = self.tm_a_u + self.aw
        self.tm_b_q = self.tm_a_v + self.aw
        self.tm_b_k = self.tm_b_q + self.bw
        self.tm_b_v = self.tm_b_k + self.kvw
        self.tm_c_k = self.tm_b_v + self.kvw
        self.tm_c_o = self.tm_c_k + self.cw
        self.tm_width = self.tm_c_o + self.cw
        self.fm_c_q = 0
        self.fm_c_v = self.fm_c_q + self.cw
        self.fm_g = self.fm_c_v + self.cw
        self.fm_rows = self.fm_g + 2 * SUBLANES
        assert self.kvw == LANES and 2 * self.ch == SUBLANES and self.grp % 2 == 0
        assert self.fm_rows % BF16_ROWS == 0 and self.ch % 2 == 0


def _scan_lanes(x, length, op, fill):
    outs = []
    for j in range(x.shape[1] // LANES):
        xj = x[:, j * LANES:(j + 1) * LANES]
        pos = lax.broadcasted_iota(jnp.int32, xj.shape, 1) & (length - 1)
        s = 1
        while s < length:
            xj = op(xj, jnp.where(pos >= s, pltpu.roll(xj, s, 1), fill))
            s *= 2
        outs.append(xj)
    return outs[0] if len(outs) == 1 else jnp.concatenate(outs, axis=1)


def _rope(xt, cos_t, sin_a, sin_b, nb, t):
    up = pltpu.roll(xt, LANES - ROT_DIM // 2, 1)
    dn = pltpu.roll(xt, ROT_DIM // 2, 1)
    shp = (nb, t, LANES)
    out = (xt.reshape(shp) * cos_t[None] + up.reshape(shp) * sin_a[None]
           + dn.reshape(shp) * sin_b[None])
    return out.reshape(nb * t, LANES)


def _last_lane(row, length):
    lane = lax.broadcasted_iota(jnp.int32, row.shape, 1)
    return jnp.max(jnp.where(lane == length - 1, row, -jnp.inf), axis=-1, keepdims=True)


def _mlstm_chunk(k_pair, odd, q_t, v_t, b_row, r_row, cm_row, r_full, state, m_prev, causal_t):
    L = q_t.shape[1]
    zeros = jnp.zeros_like(q_t)
    q_ext = jnp.concatenate([zeros, q_t] if odd else [q_t, zeros], axis=0).astype(BF16)
    g_row = jnp.maximum(m_prev[:, :L], cm_row)
    m_t = g_row + b_row
    p_t = jnp.where(causal_t, jnp.exp(r_full[:, :L] - g_row), 0.0)
    s_t = (_dot(k_pair, q_ext) * p_t).astype(BF16)
    v_ext = jnp.concatenate([v_t, jnp.ones_like(v_t)], axis=0)
    intra = _dot(v_ext.astype(BF16), s_t)
    inter = _dot(state.astype(BF16), q_ext)
    comb = jnp.exp(m_prev[:, :L] - g_row) * inter + intra
    den = comb[HEAD_DIM:HEAD_DIM + 1, :]
    h_t = comb[:HEAD_DIM, :] / jnp.maximum(jnp.abs(den), jnp.exp(-m_t))
    m_new = _last_lane(m_t, L)
    b_last = _last_lane(b_row, L)
    decay = jnp.exp(b_last + m_prev - m_new)
    w_row = jnp.exp(r_row + (b_last - m_new))
    upd = _dot((v_ext * w_row).astype(BF16), k_pair)
    lane = lax.broadcasted_iota(jnp.int32, upd.shape, 1)
    own = (lane >= HEAD_DIM) if odd else (lane < HEAD_DIM)
    state_new = decay * state + jnp.where(own, upd, 0.0)
    return h_t, state_new, jnp.broadcast_to(m_new, (1, LANES))


def _mixer_kernel(dm, prompt, layer, nb, t, mchunk, *refs):
    if prompt:
        (x_ref, cos_ref, sa_ref, sb_ref, ng_ref, wtm_ref, wfm_ref, gb_ref, gg_ref, gw_ref, gbx_ref,
         sinks_all, wout_ref,
         x1_ref, klast_ref, vlast_ref, cout_ref, nout_ref, mm_ref,
         kbuf, vbuf, mixed_ref, ht_ref, mst_ref) = refs
    else:
        (x_ref, cos_ref, sa_ref, sb_ref, ng_ref, wtm_ref, wfm_ref, gb_ref, gg_ref, gw_ref, gbx_ref,
         sinks_all, wout_ref, ck_ref, cv_ref, c0_ref, n0_ref, m0_all,
         x1_ref, av_ref, knew_ref, vnew_ref, cout_ref, nout_ref, mm_ref,
         kbuf, vbuf, mixed_ref, ht_ref, mst_ref) = refs
    d_model = dm.d
    rows = nb * t
    win = 2 * CHUNK
    step = pl.program_id(1) if prompt else None

    x = x_ref[...].reshape(rows, d_model)
    xn = _rmsnorm(x, ng_ref[...]).astype(BF16)

    def sec(off, width):
        return _dot(xn, wtm_ref[:, off:off + width])

    cos_t, sin_a, sin_b = cos_ref[...], sa_ref[...], sb_ref[...]

    a_u = _gelu(sec(dm.tm_a_u, dm.aw))
    a_v = _rmsnorm(_gelu(sec(dm.tm_a_v, dm.aw)), gg_ref[...])
    zq = sec(dm.tm_b_q, dm.bw)
    q_tiles = [_rope(zq[:, j * LANES:(j + 1) * LANES], cos_t, sin_a, sin_b, nb, t) * ATTN_SCALE
               for j in range(dm.bw // LANES)]
    k_rot = _rope(sec(dm.tm_b_k, dm.kvw), cos_t, sin_a, sin_b, nb, t)
    v_new = sec(dm.tm_b_v, dm.kvw)
    c_k = (sec(dm.tm_c_k, dm.cw) * ATTN_SCALE).astype(BF16)
    c_o = jax.nn.sigmoid(sec(dm.tm_c_o, dm.cw))
    fm = jnp.concatenate([_dot_nt(wfm_ref[0:dm.fm_c_v, :], xn),
                          _dot_nt(wfm_ref[dm.fm_c_v:dm.fm_rows, :], xn)], axis=0)
    g_if = fm[dm.fm_g:dm.fm_g + SUBLANES, :] + gb_ref[0:SUBLANES, 0:1]
    g_fi = fm[dm.fm_g + SUBLANES:dm.fm_g + 2 * SUBLANES, :] + gb_ref[SUBLANES:2 * SUBLANES, 0:1]
    cum_b = _scan_lanes(_log_sigmoid(g_fi), mchunk, jnp.add, 0.0)
    r_all = g_if - cum_b
    cm_all = _scan_lanes(r_all, mchunk, jnp.maximum, -jnp.inf)

    lane_kv = lax.broadcasted_iota(jnp.int32, (rows, LANES), 1)
    k_sw, v_sw = pltpu.roll(k_rot, HEAD_DIM, 1), pltpu.roll(v_new, HEAD_DIM, 1)
    ones_t = jnp.ones((t, LANES), BF16)

    def dup(a, a_sw, g):
        lo = lane_kv < HEAD_DIM
        return (jnp.where(lo, a, a_sw) if g == 0 else jnp.where(lo, a_sw, a)).astype(BF16)

    if prompt:
        @pl.when(step == 0)
        def _():
            kbuf[:, :, 0:win, :] = jnp.zeros((nb, dm.kv, win, LANES), BF16)
            vbuf[:, :, 0:win, :] = jnp.zeros((nb, dm.kv, win, 2 * LANES), BF16)
            mst_ref[...] = jnp.zeros(mst_ref.shape, F32)
            mm_ref[...] = jnp.zeros(mm_ref.shape, F32)
    else:
        lane_c = lax.broadcasted_iota(jnp.int32, (win, LANES), 1) < HEAD_DIM
        ones_w = jnp.ones((win, LANES), BF16)
        for bi in range(nb):
            ck, cv = ck_ref[bi], cv_ref[bi]
            ck_sw, cv_sw = pltpu.roll(ck, HEAD_DIM, 1), pltpu.roll(cv, HEAD_DIM, 1)
            for g in range(dm.kv):
                ka, kb = (ck, ck_sw) if g == 0 else (ck_sw, ck)
                va, vb = (cv, cv_sw) if g == 0 else (cv_sw, cv)
                kbuf[bi, g, 0:win, :] = jnp.where(lane_c, ka, kb).astype(BF16)
                vbuf[bi, g, 0:win, 0:LANES] = jnp.where(lane_c, va, vb).astype(BF16)
                vbuf[bi, g, 0:win, LANES:2 * LANES] = ones_w
    for g in range(dm.kv):
        kd, vd = dup(k_rot, k_sw, g), dup(v_new, v_sw, g)
        for bi in range(nb):
            kbuf[bi, g, win:win + t, :] = kd[bi * t:(bi + 1) * t]
            vbuf[bi, g, win:win + t, 0:LANES] = vd[bi * t:(bi + 1) * t]
            vbuf[bi, g, win:win + t, LANES:2 * LANES] = ones_t

    blk = min(A_BLOCK, t)
    ri = lax.broadcasted_iota(jnp.int32, (A_BLOCK, A_BLOCK), 0)
    ci = lax.broadcasted_iota(jnp.int32, (A_BLOCK, A_BLOCK), 1)
    block_causal = (ci // CHUNK) <= (ri // CHUNK)
    w_sp = [jnp.where(block_causal, gw_ref[h], 0.0)[:blk, :blk].astype(BF16) for h in range(dm.ah)]
    lane_a = lax.broadcasted_iota(jnp.int32, (blk, dm.aw), 1) // HEAD_DIM
    gbx = gbx_ref[0:blk, :]
    for r0 in range(0, rows, blk):
        vb = a_v[r0:r0 + blk].astype(BF16)
        mixed = _dot(w_sp[dm.ah - 1], vb)
        for h in range(dm.ah - 2, -1, -1):
            mixed = jnp.where(lane_a == h, _dot(w_sp[h], vb), mixed)
        mixed_ref[r0:r0 + blk, 0:dm.aw] = a_u[r0:r0 + blk] * (mixed + gbx)

    lane_q = lax.broadcasted_iota(jnp.int32, (CHUNK, LANES), 1) < HEAD_DIM
    key_lane = lax.broadcasted_iota(jnp.int32, (1, win + CHUNK), 1)
    for bi in range(nb):
        for c in range(t // CHUNK):
            r0 = bi * t + c * CHUNK
            key_ok = None
            if prompt and c < win // CHUNK:
                key_ok = (step * t + c * CHUNK - win + key_lane) >= 0
            for g in range(dm.kv):
                parts = []
                for j in range(dm.grp):
                    tile = q_tiles[(g * dm.grp + j) // 2][r0:r0 + CHUNK]
                    parts.append(jnp.where(lane_q if j % 2 == 0 else ~lane_q, tile, 0.0))
                q4 = jnp.concatenate(parts, axis=0).astype(BF16)
                ks = slice(c * CHUNK, c * CHUNK + win + CHUNK)
                s = _dot_nt(q4, kbuf[bi, g, ks, :])
                if key_ok is not None:
                    s = jnp.where(key_ok, s, -jnp.inf)
                p_parts, mx_parts = [], []
                for j in range(dm.grp):
                    sj = s[j * CHUNK:(j + 1) * CHUNK]
                    mxj = jnp.maximum(jnp.max(sj, axis=-1, keepdims=True),
                                      sinks_all[layer, g * dm.grp + j])
                    p_parts.append(jnp.exp(sj - mxj))
                    mx_parts.append(mxj)
                p = jnp.concatenate(p_parts, axis=0).astype(BF16)
                o2 = _dot(p, vbuf[bi, g, ks, :])
                outs = []
                for j in range(dm.grp):
                    rs = slice(j * CHUNK, (j + 1) * CHUNK)
                    den = o2[rs, LANES:2 * LANES] + jnp.exp(
                        sinks_all[layer, g * dm.grp + j] - mx_parts[j])
                    outs.append(o2[rs, 0:LANES] / den)
                for i in range(dm.grp // 2):
                    col = dm.aw + (g * dm.grp + 2 * i) * HEAD_DIM
                    mixed_ref[r0:r0 + CHUNK, col:col + LANES] = jnp.where(
                        lane_q, outs[2 * i], outs[2 * i + 1])

    si = lax.broadcasted_iota(jnp.int32, (mchunk, mchunk), 0)
    ti = lax.broadcasted_iota(jnp.int32, (mchunk, mchunk), 1)
    causal_t = si <= ti
    for h in range(dm.ch):
        tile = slice((h // 2) * LANES, (h // 2 + 1) * LANES)
        q_h = fm[dm.fm_c_q + h * HEAD_DIM:dm.fm_c_q + (h + 1) * HEAD_DIM, :]
        v_h = fm[dm.fm_c_v + h * HEAD_DIM:dm.fm_c_v + (h + 1) * HEAD_DIM, :]
        r_full = jnp.broadcast_to(r_all[h:h + 1, :], (LANES, rows)).T
        for bi in range(nb):
            if prompt:
                state, m_prev = mst_ref[bi, h], mm_ref[bi, h:h + 1, :]
            else:
                cn = jnp.concatenate(
                    [c0_ref[bi, h], jnp.broadcast_to(n0_ref[bi, h:h + 1, :], (HEAD_DIM, HEAD_DIM))],
                    axis=0)
                zero = jnp.zeros_like(cn)
                state = jnp.concatenate([zero, cn] if h % 2 == 1 else [cn, zero], axis=1)
                m_prev = jnp.full((1, LANES), m0_all[layer, pl.program_id(0) * nb + bi, h], F32)
            for c in range(t // mchunk):
                r0 = bi * t + c * mchunk
                rs = slice(r0, r0 + mchunk)
                h_t, state, m_prev = _mlstm_chunk(
                    c_k[rs, tile], h % 2 == 1, q_h[:, rs], v_h[:, rs],
                    cum_b[h:h + 1, rs], r_all[h:h + 1, rs], cm_all[h:h + 1, rs], r_full[rs],
                    state, m_prev, causal_t)
                ht_ref[h * HEAD_DIM:(h + 1) * HEAD_DIM, rs] = h_t
            if prompt:
                mst_ref[bi, h] = state
            mm_ref[bi, h:h + 1, :] = m_prev
            own = state[:, HEAD_DIM:] if h % 2 == 1 else state[:, :HEAD_DIM]
            cout_ref[bi, h] = own[:HEAD_DIM]
            nout_ref[bi, h:h + 1, :] = own[HEAD_DIM:HEAD_DIM + 1]
    col = dm.aw + dm.bw
    mixed_ref[:, col:col + dm.cw] = c_o * ht_ref[...].T

    x1 = x + _dot(mixed_ref[...].astype(BF16), wout_ref[...])
    x1_ref[...] = x1.reshape(nb, t, d_model)

    if prompt:
        @pl.when(step == pl.num_programs(1) - 1)
        def _():
            klast_ref[0] = k_rot[t - win:t]
            vlast_ref[0] = v_new[t - win:t]

        for g in range(dm.kv):
            kbuf[0, g, 0:win, :] = kbuf[0, g, t:t + win, :]
            vbuf[0, g, 0:win, :] = vbuf[0, g, t:t + win, :]
    else:
        av_ref[...] = a_v.reshape(nb, t, dm.aw)
        knew_ref[...] = k_rot.reshape(nb, t, dm.kvw)
        vnew_ref[...] = v_new.reshape(nb, t, dm.kvw)


def _layer_spec(layer, shape):
    idx = (layer,) + (0,) * len(shape)
    return pl.BlockSpec((None,) + tuple(shape), lambda *_: idx, pipeline_mode=pl.Buffered(1))


def _mixer_call(dm, prompt, layer, x, tables, wts, caches=None):
    batch, seq, d_model = x.shape
    if prompt:
        nb, t, mchunk = 1, min(PROMPT_TILE, seq), min(MLSTM_CHUNK, seq)
        grid = (batch, seq // t)
        bmap = lambda b, s: (b, s, 0)
        smap = lambda b, s: (b, 0, 0)
        smap4 = lambda b, s: (b, 0, 0, 0)
        tmap = lambda b, s: (s, 0)
        sem = ("arbitrary", "arbitrary")
    else:
        nb, t, mchunk = min(SAMPLE_ROWS, batch), seq, seq
        grid = (batch // nb,)
        bmap = lambda b: (b, 0, 0)
        smap = bmap
        smap4 = lambda b: (b, 0, 0, 0)
        tmap = lambda b: (0, 0)
        sem = ("arbitrary",)
    assert seq % t == 0 and t % mchunk == 0 and t % CHUNK == 0 and (nb * t) % LANES == 0
    assert batch % nb == 0 and mchunk & (mchunk - 1) == 0 and mchunk <= LANES
    win = 2 * CHUNK

    in_specs = [
        pl.BlockSpec((nb, t, d_model), bmap),
        pl.BlockSpec((t, LANES), tmap), pl.BlockSpec((t, LANES), tmap), pl.BlockSpec((t, LANES), tmap),
        _layer_spec(layer, (1, d_model)),
        _layer_spec(layer, (d_model, dm.tm_width)),
        _layer_spec(layer, (dm.fm_rows, d_model)),
        _layer_spec(layer, (2 * SUBLANES, LANES)),
        _layer_spec(layer, (1, dm.aw)),
        _layer_spec(layer, (dm.ah, A_BLOCK, A_BLOCK)),
        _layer_spec(layer, (A_BLOCK, dm.aw)),
        pl.BlockSpec(memory_space=pltpu.SMEM),
        _layer_spec(layer, (d_model, d_model)),
    ]
    args = [x, *tables, wts["attn_norm"], wts["w_tm"], wts["w_fm"], wts["gate_b"], wts["gmlp_norm"],
            wts["gmlp_w"], wts["gmlp_bx"], wts["sinks"], wts["w_out"]]
    state_shapes = [jax.ShapeDtypeStruct((batch, dm.ch, HEAD_DIM, HEAD_DIM), F32),
                    jax.ShapeDtypeStruct((batch, dm.ch, HEAD_DIM), F32),
                    jax.ShapeDtypeStruct((batch, SUBLANES, LANES), F32)]
    state_specs = [pl.BlockSpec((nb, dm.ch, HEAD_DIM, HEAD_DIM), smap4),
                   pl.BlockSpec((nb, dm.ch, HEAD_DIM), smap),
                   pl.BlockSpec((nb, SUBLANES, LANES), smap)]
    if prompt:
        out_shape = [jax.ShapeDtypeStruct((batch, seq, d_model), F32),
                     jax.ShapeDtypeStruct((batch, win, dm.kvw), F32),
                     jax.ShapeDtypeStruct((batch, win, dm.kvw), F32)] + state_shapes
        out_specs = [pl.BlockSpec((nb, t, d_model), bmap),
                     pl.BlockSpec((1, win, dm.kvw), smap),
                     pl.BlockSpec((1, win, dm.kvw), smap)] + state_specs
    else:
        ck, cv, c0, n0, m0 = caches
        lmap = lambda b: (layer, b, 0, 0)
        in_specs += [pl.BlockSpec((None, nb, win, dm.kvw), lmap),
                     pl.BlockSpec((None, nb, win, dm.kvw), lmap),
                     pl.BlockSpec((None, nb, dm.ch, HEAD_DIM, HEAD_DIM), lambda b: (layer, b, 0, 0, 0)),
                     pl.BlockSpec((None, nb, dm.ch, HEAD_DIM), lmap),
                     pl.BlockSpec(memory_space=pltpu.SMEM)]
        args += [ck, cv, c0, n0, m0]
        out_shape = [jax.ShapeDtypeStruct((batch, seq, d_model), F32),
                     jax.ShapeDtypeStruct((batch, seq, dm.aw), F32),
                     jax.ShapeDtypeStruct((batch, seq, dm.kvw), F32),
                     jax.ShapeDtypeStruct((batch, seq, dm.kvw), F32)] + state_shapes
        out_specs = [pl.BlockSpec((nb, t, d_model), bmap),
                     pl.BlockSpec((nb, t, dm.aw), bmap),
                     pl.BlockSpec((nb, t, dm.kvw), bmap),
                     pl.BlockSpec((nb, t, dm.kvw), bmap)] + state_specs
    scratch = [pltpu.VMEM((nb, dm.kv, win + t, LANES), BF16),
               pltpu.VMEM((nb, dm.kv, win + t, 2 * LANES), BF16),
               pltpu.VMEM((nb * t, d_model), F32),
               pltpu.VMEM((dm.cw, nb * t), F32),
               pltpu.VMEM((nb, dm.ch, LANES, LANES), F32)]
    return pl.pallas_call(
        functools.partial(_mixer_kernel, dm, prompt, layer, nb, t, mchunk),
        grid=grid, in_specs=in_specs, out_specs=out_specs, out_shape=out_shape,
        scratch_shapes=scratch,
        compiler_params=pltpu.CompilerParams(dimension_semantics=sem,
                                             vmem_limit_bytes=VMEM_LIMIT_BYTES),
    )(*args)


def _ffn_kernel(prompt, final, nb, t, d_ff, *refs):
    if prompt:
        (x_ref, g_ref, wup_ref, cw_ref, cb_ref, wdn_ref, fin_ref,
         y_ref, cst_ref, hist_ref, gbuf) = refs
    else:
        (x_ref, g_ref, wup_ref, cw_ref, cb_ref, wdn_ref, fin_ref, cc_ref,
         y_ref, cst_ref, hist_ref, gbuf) = refs
    d_model = x_ref.shape[-1]
    rows = nb * t
    hist = CONV_W - 1
    assert hist == 2

    x = x_ref[...].reshape(rows, d_model)
    xn = _rmsnorm(x, g_ref[...]).astype(BF16)

    if prompt:
        @pl.when(pl.program_id(1) == 0)
        def _():
            hist_ref[...] = jnp.zeros(hist_ref.shape, F32)
    else:
        for bi in range(nb):
            hist_ref[bi] = jnp.concatenate(
                [jnp.zeros((SUBLANES - hist, 2 * d_ff), F32), cc_ref[bi]], axis=0)

    row8 = lax.broadcasted_iota(jnp.int32, (SUBLANES, FFN_COLS), 0)
    n_blocks = d_ff // FFN_COLS
    y = x
    k0 = 0
    for j in range(n_blocks):
        conv = []
        for half in range(2):
            c0 = half * d_ff + j * FFN_COLS
            cs = slice(c0, c0 + FFN_COLS)
            up = _dot(xn, wup_ref[:, cs])
            w0, w1, w2 = cw_ref[0:1, cs], cw_ref[1:2, cs], cw_ref[2:3, cs]
            parts = []
            for bi in range(nb):
                ub = up[bi * t:(bi + 1) * t]
                prev = hist_ref[bi, :, cs]
                r1, r2 = pltpu.roll(ub, 1, 0), pltpu.roll(ub, 2, 0)
                h1 = jnp.where(row8 < 1, pltpu.roll(prev, 1, 0), r1[0:SUBLANES])
                h2 = jnp.where(row8 < 2, pltpu.roll(prev, 2, 0), r2[0:SUBLANES])
                s1 = jnp.concatenate([h1, r1[SUBLANES:]], axis=0)
                s2 = jnp.concatenate([h2, r2[SUBLANES:]], axis=0)
                parts.append(cb_ref[:, cs] + s2 * w0 + s1 * w1 + ub * w2)
                hist_ref[bi, :, cs] = ub[t - SUBLANES:t]
                cst_ref[bi, :, cs] = ub[t - hist:t]
            conv.append(parts[0] if nb == 1 else jnp.concatenate(parts, axis=0))
        gbuf[:, j * FFN_COLS:(j + 1) * FFN_COLS] = (_gelu(conv[0]) * conv[1]).astype(BF16)
        if (j + 1) % FFN_DOWN_GROUP == 0 or j == n_blocks - 1:
            k1 = (j + 1) * FFN_COLS
            y = y + _dot(gbuf[:, k0:k1], wdn_ref[k0:k1, :])
            k0 = k1
    if final:
        y = _rmsnorm(y, fin_ref[...])
    y_ref[...] = y.reshape(nb, t, d_model)


def _ffn_call(prompt, final, layer, x, wts, final_norm, cache_conv=None):
    batch, seq, d_model = x.shape
    d_ff = wts["w_down"].shape[1]
    if prompt:
        nb, t = 1, min(FFN_TILE, seq)
        grid = (batch, seq // t)
        bmap = lambda b, s: (b, s, 0)
        smap = lambda b, s: (b, 0, 0)
        sem = ("arbitrary", "arbitrary")
    else:
        nb, t = min(SAMPLE_ROWS, batch), seq
        grid = (batch // nb,)
        bmap = lambda b: (b, 0, 0)
        smap = bmap
        sem = ("arbitrary",)
    assert seq % t == 0 and t % SUBLANES == 0 and t > SUBLANES and batch % nb == 0
    assert d_ff % FFN_COLS == 0
    hist = CONV_W - 1
    in_specs = [
        pl.BlockSpec((nb, t, d_model), bmap),
        _layer_spec(layer, (1, d_model)),
        _layer_spec(layer, (d_model, 2 * d_ff)),
        _layer_spec(layer, (CONV_W, 2 * d_ff)),
        _layer_spec(layer, (1, 2 * d_ff)),
        _layer_spec(layer, (d_ff, d_model)),
        _layer_spec(0, (1, d_model)),
    ]
    args = [x, wts["ffn_norm"], wts["w_up"], wts["conv_w"], wts["conv_b"], wts["w_down"],
            final_norm]
    if not prompt:
        in_specs.append(pl.BlockSpec((None, nb, hist, 2 * d_ff), lambda b: (layer, b, 0, 0)))
        args.append(cache_conv)
    out_shape = [jax.ShapeDtypeStruct((batch, seq, d_model), F32),
                 jax.ShapeDtypeStruct((batch, hist, 2 * d_ff), F32)]
    out_specs = [pl.BlockSpec((nb, t, d_model), bmap),
                 pl.BlockSpec((nb, hist, 2 * d_ff), smap)]
    scratch = [pltpu.VMEM((nb, SUBLANES, 2 * d_ff), F32),
               pltpu.VMEM((nb * t, d_ff), BF16)]
    return pl.pallas_call(
        functools.partial(_ffn_kernel, prompt, final, nb, t, d_ff),
        grid=grid, in_specs=in_specs, out_specs=out_specs, out_shape=out_shape,
        scratch_shapes=scratch,
        compiler_params=pltpu.CompilerParams(dimension_semantics=sem,
                                             vmem_limit_bytes=VMEM_LIMIT_BYTES),
    )(*args)


def _rope_tables(pos):
    half = ROT_DIM // 2
    inv_freq = ROPE_THETA ** (-jnp.arange(half, dtype=F32) / half)
    ang = pos.astype(F32)[:, None] * inv_freq[None, :]
    cos, sin = jnp.cos(ang), jnp.sin(ang)
    ones = jnp.ones((pos.shape[0], HEAD_DIM - ROT_DIM), F32)
    zeros = jnp.zeros((pos.shape[0], HEAD_DIM - ROT_DIM), F32)
    zh = jnp.zeros_like(sin)
    cos_t = jnp.concatenate([cos, cos, ones], axis=1)
    sin_a = jnp.concatenate([-sin, zh, zeros], axis=1)
    sin_b = jnp.concatenate([zh, sin, zeros], axis=1)
    rep = LANES // HEAD_DIM
    return tuple(jnp.tile(a, (1, rep)) for a in (cos_t, sin_a, sin_b))


def _prep_weights(dm, attn_norm, w_in, gmlp_norm, gmlp_w, gmlp_b, attn_sinks, mlstm_gate_b,
                  w_out, ffn_norm, w_up, conv_w, conv_b, w_down):
    def cols(off, width):
        return w_in[:, :, off:off + width]

    w_tm = jnp.concatenate([cols(dm.off_a_u, dm.aw), cols(dm.off_a_v, dm.aw), cols(dm.off_b_q, dm.bw),
                            cols(dm.off_b_k, dm.kvw), cols(dm.off_b_v, dm.kvw), cols(dm.off_c_k, dm.cw),
                            cols(dm.off_c_o, dm.cw)], axis=2)
    g_i, g_f = cols(dm.off_c_g, dm.ch), cols(dm.off_c_g + dm.ch, dm.ch)
    w_fm = jnp.swapaxes(
        jnp.concatenate([cols(dm.off_c_q, dm.cw), cols(dm.off_c_v, dm.cw), g_i, g_f, g_f, g_i], axis=2),
        1, 2)
    b_i, b_f = mlstm_gate_b[:, :dm.ch], mlstm_gate_b[:, dm.ch:]
    gate_b = jnp.concatenate([b_i, b_f, b_f, b_i], axis=1)
    depth = w_in.shape[0]
    return {
        "attn_norm": attn_norm[:, None, :],
        "w_tm": w_tm.astype(BF16),
        "w_fm": w_fm.astype(BF16),
        "gate_b": jnp.broadcast_to(gate_b[:, :, None], (depth, 2 * SUBLANES, LANES)),
        "gmlp_norm": gmlp_norm[:, None, :],
        "gmlp_w": gmlp_w,
        "gmlp_bx": jnp.repeat(jnp.swapaxes(gmlp_b, 1, 2), HEAD_DIM, axis=2),
        "sinks": attn_sinks,
        "w_out": w_out.astype(BF16),
        "ffn_norm": ffn_norm[:, None, :],
        "w_up": w_up.astype(BF16),
        "conv_w": conv_w,
        "conv_b": conv_b[:, None, :],
        "w_down": w_down.astype(BF16),
    }


def kernel(x_prompt, x_sample, cache_k, cache_v, state_C, state_n, state_m, cache_conv, attn_norm, w_in, gmlp_norm, gmlp_w, gmlp_b, attn_sinks, mlstm_gate_b, w_out, ffn_norm, w_up, conv_w, conv_b, w_down, final_norm):
    depth = w_in.shape[0]
    d_model = x_prompt.shape[-1]
    kv_heads = cache_k.shape[3]
    dm = _Dims(d_model, kv_heads)
    assert w_in.shape[-1] == dm.in_width
    bp, sp = x_prompt.shape[:2]
    bs, ts = x_sample.shape[:2]

    tab_p = _rope_tables(jnp.arange(sp, dtype=jnp.int32))
    tab_s = _rope_tables(PAST_LEN + jnp.arange(ts, dtype=jnp.int32))
    fin = final_norm[None, None, :]
    wts = _prep_weights(dm, attn_norm, w_in, gmlp_norm, gmlp_w, gmlp_b, attn_sinks, mlstm_gate_b,
                        w_out, ffn_norm, w_up, conv_w, conv_b, w_down)
    caches = (cache_k.reshape(depth, bs, -1, dm.kvw), cache_v.reshape(depth, bs, -1, dm.kvw),
              state_C, state_n, state_m)

    xp, xs = x_prompt, x_sample
    p_st, s_st = [], []
    for l in range(depth):
        last = l == depth - 1
        x1, k_last, v_last, c_p, n_p, mm = _mixer_call(dm, True, l, xp, tab_p, wts)
        xp, conv_p = _ffn_call(True, last, l, x1, wts, fin)
        p_st.append((k_last.reshape(bp, -1, kv_heads, HEAD_DIM), v_last.reshape(bp, -1, kv_heads, HEAD_DIM),
                     c_p, n_p, mm[:, :dm.ch, 0], conv_p))
        x1, a_v, k_new, v_new, c_s, n_s, mm = _mixer_call(dm, False, l, xs, tab_s, wts, caches)
        xs, conv_s = _ffn_call(False, last, l, x1, wts, fin, cache_conv)
        s_st.append((a_v, k_new.reshape(bs, ts, kv_heads, HEAD_DIM), v_new.reshape(bs, ts, kv_heads, HEAD_DIM),
                     c_s, n_s, mm[:, :dm.ch, 0], conv_s))

    def stk(states, i):
        return jnp.stack([s[i] for s in states], axis=0)

    return (xp, xs,
            stk(p_st, 0), stk(p_st, 1), stk(p_st, 2), stk(p_st, 3), stk(p_st, 4), stk(p_st, 5),
            stk(s_st, 0), stk(s_st, 1), stk(s_st, 2), stk(s_st, 3), stk(s_st, 4), stk(s_st, 5), stk(s_st, 6))
```

```python
import functools

import numpy as np
import jax
import jax.numpy as jnp
from jax import lax
from jax.experimental import pallas as pl
from jax.experimental.pallas import tpu as pltpu

HEAD_DIM = 64
CHUNK = 64
A_BLOCK = 128
ROT_DIM = 16
ROPE_THETA = 500000.0
ATTN_SCALE = HEAD_DIM ** -0.5
RMS_EPS = 1e-6
CONV_W = 3
PAST_LEN = 1024

LANES = 128
SUBLANES = 8
BF16_ROWS = 16
VMEM_LIMIT_BYTES = 56 * 1024 * 1024

PROMPT_TILE = 512
FFN_TILE = 1024
FFN_DOWN_GROUP = 4
MLSTM_CHUNK = 128
SAMPLE_ROWS = 8
FFN_COLS = 256

F32 = jnp.float32
BF16 = jnp.bfloat16


def _dot(a, b):
    return jnp.dot(a, b, preferred_element_type=F32)


def _dot_nt(a, b):
    return lax.dot_general(a, b, (((1,), (1,)), ((), ())), preferred_element_type=F32)


def _rmsnorm(x, g):
    ms = jnp.mean(x * x, axis=-1, keepdims=True)
    return x * lax.rsqrt(ms + RMS_EPS) * g


def _gelu(x):
    return 0.5 * x * (1.0 + lax.erf(x * np.float32(np.sqrt(0.5))))


def _log_sigmoid(x):
    return jnp.minimum(x, 0.0) - jnp.log1p(jnp.exp(-jnp.abs(x)))


class _Dims:
    def __init__(self, d_model, kv_heads):
        self.d = d_model
        self.aw = d_model // 4
        self.ah = self.aw // HEAD_DIM
        self.bw = d_model // 2
        self.bh = self.bw // HEAD_DIM
        self.kv = kv_heads
        self.grp = self.bh // kv_heads
        self.kvw = kv_heads * HEAD_DIM
        self.cw = d_model // 4
        self.ch = self.cw // HEAD_DIM
        self.off_a_u = 0
        self.off_a_v = self.off_a_u + self.aw
        self.off_b_q = self.off_a_v + self.aw
        self.off_b_k = self.off_b_q + self.bw
        self.off_b_v = self.off_b_k + self.kvw
        self.off_c_q = self.off_b_v + self.kvw
        self.off_c_k = self.off_c_q + self.cw
        self.off_c_v = self.off_c_k + self.cw
        self.off_c_o = self.off_c_v + self.cw
        self.off_c_g = self.off_c_o + self.cw
        self.in_width = self.off_c_g + 2 * self.ch
        self.tm_a_u = 0
        self.tm_a_v = self.tm_a_u + self.aw
        self.tm_b_q = self.tm_a_v + self.aw
        self.tm_b_k = self.tm_b_q + self.bw
        self.tm_b_v = self.tm_b_k + self.kvw
        self.tm_c_k = self.tm_b_v + self.kvw
        self.tm_c_o = self.tm_c_k + self.cw
        self.tm_width = self.tm_c_o + self.cw
        self.fm_c_q = 0
        self.fm_c_v = self.fm_c_q + self.cw
        self.fm_g = self.fm_c_v + self.cw
        self.fm_rows = self.fm_g + 2 * SUBLANES
        assert self.kvw == LANES and 2 * self.ch == SUBLANES and self.grp % 2 == 0
        assert self.fm_rows % BF16_ROWS == 0 and self.ch % 2 == 0


def _scan_lanes(x, length, op, fill):
    outs = []
    for j in range(x.shape[1] // LANES):
        xj = x[:, j * LANES:(j + 1) * LANES]
        pos = lax.broadcasted_iota(jnp.int32, xj.shape, 1) & (length - 1)
        s = 1
        while s < length:
            xj = op(xj, jnp.where(pos >= s, pltpu.roll(xj, s, 1), fill))
            s *= 2
        outs.append(xj)
    return outs[0] if len(outs) == 1 else jnp.concatenate(outs, axis=1)


def _rope(xt, cos_t, sin_a, sin_b, nb, t):
    up = pltpu.roll(xt, LANES - ROT_DIM // 2, 1)
    dn = pltpu.roll(xt, ROT_DIM // 2, 1)
    shp = (nb, t, LANES)
    out = (xt.reshape(shp) * cos_t[None] + up.reshape(shp) * sin_a[None]
           + dn.reshape(shp) * sin_b[None])
    return out.reshape(nb * t, LANES)


def _last_lane(rows, length):
    lane = lax.broadcasted_iota(jnp.int32, rows.shape, 1)
    return jnp.max(jnp.where(lane == length - 1, rows, -jnp.inf), axis=-1, keepdims=True)


def _mlstm_pair(k_pairs, q_ts, v_ts, b_rows, r_rows, cm_rows, b_lasts, cm_lasts, r_fulls,
                states, m_prevs, causal_t):
    n_chunks = len(k_pairs)
    L = q_ts[0][0].shape[1]
    zeros = jnp.zeros_like(q_ts[0][0])
    lane = lax.broadcasted_iota(jnp.int32, (LANES, LANES), 1)
    own = [lane < HEAD_DIM, lane >= HEAD_DIM]
    m_cur = list(m_prevs)
    pre = []
    for c in range(n_chunks):
        q_exts = [jnp.concatenate([q_ts[c][0], zeros], axis=0),
                  jnp.concatenate([zeros, q_ts[c][1]], axis=0)]
        s_both = _dot(k_pairs[c], jnp.concatenate(q_exts, axis=1).astype(BF16))
        heads, weighted = [], []
        for i in range(2):
            m_prev = m_cur[i]
            g_row = jnp.maximum(m_prev[:, :L], cm_rows[c][i])
            m_t = g_row + b_rows[c][i]
            p_t = jnp.where(causal_t, jnp.exp(r_fulls[c][i][:, :L] - g_row), 0.0)
            s_t = (s_both[:, i * L:(i + 1) * L] * p_t).astype(BF16)
            v_ext = jnp.concatenate([v_ts[c][i], jnp.ones_like(v_ts[c][i])], axis=0)
            rhs = jnp.concatenate(
                [(q_exts[i] * jnp.exp(m_prev[:, :L] - g_row)).astype(BF16), s_t], axis=0)
            m_new = jnp.maximum(m_prev, cm_lasts[c][i]) + b_lasts[c][i]
            decay = jnp.exp(b_lasts[c][i] + m_prev - m_new)
            weighted.append(v_ext * jnp.exp(r_rows[c][i] + (b_lasts[c][i] - m_new[:, :L])))
            heads.append((v_ext.astype(BF16), rhs, m_t, decay))
            m_cur[i] = m_new
        upd = _dot(jnp.concatenate(weighted, axis=0).astype(BF16), k_pairs[c])
        pre.append((heads, [jnp.where(own[0], upd[:LANES], 0.0), jnp.where(own[1], upd[LANES:], 0.0)]))
    st = list(states)
    st_in = []
    for c in range(n_chunks):
        st_in.append(list(st))
        st = [pre[c][0][i][3] * st[i] + pre[c][1][i] for i in range(2)]
    h_ts = []
    for c in range(n_chunks):
        outs = []
        for i in range(2):
            v_ext, rhs, m_t, _ = pre[c][0][i]
            comb = _dot(jnp.concatenate([st_in[c][i].astype(BF16), v_ext], axis=1), rhs)
            den = comb[HEAD_DIM:HEAD_DIM + 1, :]
            outs.append(comb[:HEAD_DIM, :] / jnp.maximum(jnp.abs(den), jnp.exp(-m_t)))
        h_ts.append(outs)
    return h_ts, st, m_cur


def _mixer_kernel(dm, prompt, layer, nb, t, mchunk, *refs):
    if prompt:
        (x_ref, cos_ref, sa_ref, sb_ref, ng_ref, wtm_ref, wfm_ref, gb_ref, gg_ref, gw_ref, gbx_ref,
         sinks_all, wout_ref,
         x1_ref, klast_ref, vlast_ref, cout_ref, nout_ref, mm_ref,
         kbuf, vbuf, mixed_ref, ht_ref, mst_ref) = refs
    else:
        (x_ref, cos_ref, sa_ref, sb_ref, ng_ref, wtm_ref, wfm_ref, gb_ref, gg_ref, gw_ref, gbx_ref,
         sinks_all, wout_ref, ck_ref, cv_ref, c0_ref, n0_ref, m0_all,
         x1_ref, av_ref, knew_ref, vnew_ref, cout_ref, nout_ref, mm_ref,
         kbuf, vbuf, mixed_ref, ht_ref, mst_ref) = refs
    d_model = dm.d
    rows = nb * t
    win = 2 * CHUNK
    step = pl.program_id(1) if prompt else None

    if prompt:
        @pl.when(step == 0)
        def _():
            kbuf[:, :, 0:win, :] = jnp.zeros((nb, dm.kv, win, LANES), BF16)
            vbuf[:, :, 0:win, :] = jnp.zeros((nb, dm.kv, win, 2 * LANES), BF16)
            mst_ref[...] = jnp.zeros(mst_ref.shape, F32)
            mm_ref[...] = jnp.zeros(mm_ref.shape, F32)

    x = x_ref[...].reshape(rows, d_model)
    xn = _rmsnorm(x, ng_ref[...]).astype(BF16)

    def sec(off, width):
        return _dot(xn, wtm_ref[:, off:off + width])

    cos_t, sin_a, sin_b = cos_ref[...], sa_ref[...], sb_ref[...]

    a_u = _gelu(sec(dm.tm_a_u, dm.aw))
    a_v = _rmsnorm(_gelu(sec(dm.tm_a_v, dm.aw)), gg_ref[...])
    zq = sec(dm.tm_b_q, dm.bw)
    q_tiles = [_rope(zq[:, j * LANES:(j + 1) * LANES], cos_t, sin_a, sin_b, nb, t) * ATTN_SCALE
               for j in range(dm.bw // LANES)]
    k_rot = _rope(sec(dm.tm_b_k, dm.kvw), cos_t, sin_a, sin_b, nb, t)
    v_new = sec(dm.tm_b_v, dm.kvw)
    c_k = (sec(dm.tm_c_k, dm.cw) * ATTN_SCALE).astype(BF16)
    c_o = jax.nn.sigmoid(sec(dm.tm_c_o, dm.cw))
    fm = jnp.concatenate([_dot_nt(wfm_ref[0:dm.fm_c_v, :], xn),
                          _dot_nt(wfm_ref[dm.fm_c_v:dm.fm_rows, :], xn)], axis=0)
    g_if = fm[dm.fm_g:dm.fm_g + SUBLANES, :] + gb_ref[0:SUBLANES, 0:1]
    g_fi = fm[dm.fm_g + SUBLANES:dm.fm_g + 2 * SUBLANES, :] + gb_ref[SUBLANES:2 * SUBLANES, 0:1]
    cum_b = _scan_lanes(_log_sigmoid(g_fi), mchunk, jnp.add, 0.0)
    r_all = g_if - cum_b
    cm_all = _scan_lanes(r_all, mchunk, jnp.maximum, -jnp.inf)

    lane_kv = lax.broadcasted_iota(jnp.int32, (rows, LANES), 1)
    k_sw, v_sw = pltpu.roll(k_rot, HEAD_DIM, 1), pltpu.roll(v_new, HEAD_DIM, 1)
    ones_t = jnp.ones((t, LANES), BF16)

    def dup(a, a_sw, g):
        lo = lane_kv < HEAD_DIM
        return (jnp.where(lo, a, a_sw) if g == 0 else jnp.where(lo, a_sw, a)).astype(BF16)

    if not prompt:
        mm_ref[...] = jnp.zeros(mm_ref.shape, F32)
        lane_c = lax.broadcasted_iota(jnp.int32, (win, LANES), 1) < HEAD_DIM
        ones_w = jnp.ones((win, LANES), BF16)
        for bi in range(nb):
            ck, cv = ck_ref[bi], cv_ref[bi]
            ck_sw, cv_sw = pltpu.roll(ck, HEAD_DIM, 1), pltpu.roll(cv, HEAD_DIM, 1)
            for g in range(dm.kv):
                ka, kb = (ck, ck_sw) if g == 0 else (ck_sw, ck)
                va, vb = (cv, cv_sw) if g == 0 else (cv_sw, cv)
                kbuf[bi, g, 0:win, :] = jnp.where(lane_c, ka, kb).astype(BF16)
                vbuf[bi, g, 0:win, 0:LANES] = jnp.where(lane_c, va, vb).astype(BF16)
                vbuf[bi, g, 0:win, LANES:2 * LANES] = ones_w
    for g in range(dm.kv):
        kd, vd = dup(k_rot, k_sw, g), dup(v_new, v_sw, g)
        for bi in range(nb):
            kbuf[bi, g, win:win + t, :] = kd[bi * t:(bi + 1) * t]
            vbuf[bi, g, win:win + t, 0:LANES] = vd[bi * t:(bi + 1) * t]
            vbuf[bi, g, win:win + t, LANES:2 * LANES] = ones_t

    blk = min(A_BLOCK, t)
    ri = lax.broadcasted_iota(jnp.int32, (A_BLOCK, A_BLOCK), 0)
    ci = lax.broadcasted_iota(jnp.int32, (A_BLOCK, A_BLOCK), 1)
    block_causal = (ci // CHUNK) <= (ri // CHUNK)
    w_sp = [jnp.where(block_causal, gw_ref[h], 0.0)[:blk, :blk].astype(BF16) for h in range(dm.ah)]
    lane_a = lax.broadcasted_iota(jnp.int32, (blk, dm.aw), 1) // HEAD_DIM
    gbx = gbx_ref[0:blk, :]
    for r0 in range(0, rows, blk):
        vb = a_v[r0:r0 + blk].astype(BF16)
        mixed = _dot(w_sp[dm.ah - 1], vb)
        for h in range(dm.ah - 2, -1, -1):
            mixed = jnp.where(lane_a == h, _dot(w_sp[h], vb), mixed)
        mixed_ref[r0:r0 + blk, 0:dm.aw] = a_u[r0:r0 + blk] * (mixed + gbx)

    lane_q = lax.broadcasted_iota(jnp.int32, (CHUNK, LANES), 1) < HEAD_DIM
    key_lane = lax.broadcasted_iota(jnp.int32, (1, win + CHUNK), 1)
    for bi in range(nb):
        for c in range(t // CHUNK):
            r0 = bi * t + c * CHUNK
            key_ok = None
            if prompt and c < win // CHUNK:
                key_ok = (step * t + c * CHUNK - win + key_lane) >= 0
            for g in range(dm.kv):
                parts = []
                for j in range(dm.grp):
                    tile = q_tiles[(g * dm.grp + j) // 2][r0:r0 + CHUNK]
                    parts.append(jnp.where(lane_q if j % 2 == 0 else ~lane_q, tile, 0.0))
                q4 = jnp.concatenate(parts, axis=0).astype(BF16)
                ks = slice(c * CHUNK, c * CHUNK + win + CHUNK)
                s = _dot_nt(q4, kbuf[bi, g, ks, :])
                if key_ok is not None:
                    s = jnp.where(key_ok, s, -jnp.inf)
                p_parts, mx_parts = [], []
                for j in range(dm.grp):
                    sj = s[j * CHUNK:(j + 1) * CHUNK]
                    mxj = jnp.maximum(jnp.max(sj, axis=-1, keepdims=True),
                                      sinks_all[layer, g * dm.grp + j])
                    p_parts.append(jnp.exp(sj - mxj))
                    mx_parts.append(mxj)
                p = jnp.concatenate(p_parts, axis=0).astype(BF16)
                o2 = _dot(p, vbuf[bi, g, ks, :])
                outs = []
                for j in range(dm.grp):
                    rs = slice(j * CHUNK, (j + 1) * CHUNK)
                    den = o2[rs, LANES:2 * LANES] + jnp.exp(
                        sinks_all[layer, g * dm.grp + j] - mx_parts[j])
                    outs.append(o2[rs, 0:LANES] / den)
                for i in range(dm.grp // 2):
                    col = dm.aw + (g * dm.grp + 2 * i) * HEAD_DIM
                    mixed_ref[r0:r0 + CHUNK, col:col + LANES] = jnp.where(
                        lane_q, outs[2 * i], outs[2 * i + 1])

    si = lax.broadcasted_iota(jnp.int32, (mchunk, mchunk), 0)
    ti = lax.broadcasted_iota(jnp.int32, (mchunk, mchunk), 1)
    causal_t = si <= ti
    n_chunks = rows // mchunk
    b_last = [_last_lane(cum_b[:, c * mchunk:(c + 1) * mchunk], mchunk) for c in range(n_chunks)]
    cm_last = [_last_lane(cm_all[:, c * mchunk:(c + 1) * mchunk], mchunk) for c in range(n_chunks)]
    for hp in range(dm.ch // 2):
        heads = (2 * hp, 2 * hp + 1)
        tile = slice(hp * LANES, (hp + 1) * LANES)
        q_hs = [fm[dm.fm_c_q + h * HEAD_DIM:dm.fm_c_q + (h + 1) * HEAD_DIM, :] for h in heads]
        v_hs = [fm[dm.fm_c_v + h * HEAD_DIM:dm.fm_c_v + (h + 1) * HEAD_DIM, :] for h in heads]
        r_fulls = [jnp.broadcast_to(r_all[h:h + 1, :], (LANES, rows)).T for h in heads]
        for bi in range(nb):
            states, m_prevs = [], []
            for i, h in enumerate(heads):
                if prompt:
                    states.append(mst_ref[bi, h])
                    m_prevs.append(mm_ref[bi, h:h + 1, :])
                else:
                    cn = jnp.concatenate(
                        [c0_ref[bi, h],
                         jnp.broadcast_to(n0_ref[bi, h:h + 1, :], (HEAD_DIM, HEAD_DIM))], axis=0)
                    zero = jnp.zeros_like(cn)
                    states.append(jnp.concatenate([zero, cn] if i == 1 else [cn, zero], axis=1))
                    m_prevs.append(jnp.full(
                        (1, LANES), m0_all[layer, pl.program_id(0) * nb + bi, h], F32))
            cis = [bi * (t // mchunk) + c for c in range(t // mchunk)]
            rss = [slice(ci * mchunk, (ci + 1) * mchunk) for ci in cis]
            h_ts, states, m_prevs = _mlstm_pair(
                [c_k[rs, tile] for rs in rss],
                [[q[:, rs] for q in q_hs] for rs in rss], [[v[:, rs] for v in v_hs] for rs in rss],
                [[cum_b[h:h + 1, rs] for h in heads] for rs in rss],
                [[r_all[h:h + 1, rs] for h in heads] for rs in rss],
                [[cm_all[h:h + 1, rs] for h in heads] for rs in rss],
                [[b_last[ci][h:h + 1] for h in heads] for ci in cis],
                [[cm_last[ci][h:h + 1] for h in heads] for ci in cis],
                [[r[rs] for r in r_fulls] for rs in rss], states, m_prevs, causal_t)
            for c, rs in enumerate(rss):
                for i, h in enumerate(heads):
                    ht_ref[h * HEAD_DIM:(h + 1) * HEAD_DIM, rs] = h_ts[c][i]
            for i, h in enumerate(heads):
                if prompt:
                    mst_ref[bi, h] = states[i]
                mm_ref[bi, h:h + 1, :] = m_prevs[i]
                own = states[i][:, HEAD_DIM:] if i == 1 else states[i][:, :HEAD_DIM]
                cout_ref[bi, h] = own[:HEAD_DIM]
                nout_ref[bi, h:h + 1, :] = own[HEAD_DIM:HEAD_DIM + 1]
    col = dm.aw + dm.bw
    mixed_ref[:, col:col + dm.cw] = c_o * ht_ref[...].T

    x1 = x + _dot(mixed_ref[...].astype(BF16), wout_ref[...])
    x1_ref[...] = x1.reshape(nb, t, d_model)

    if prompt:
        for g in range(dm.kv):
            kbuf[0, g, 0:win, :] = kbuf[0, g, t:t + win, :]
            vbuf[0, g, 0:win, :] = vbuf[0, g, t:t + win, :]
        klast_ref[0] = k_rot[t - win:t]
        vlast_ref[0] = v_new[t - win:t]
    else:
        av_ref[...] = a_v.reshape(nb, t, dm.aw)
        knew_ref[...] = k_rot.reshape(nb, t, dm.kvw)
        vnew_ref[...] = v_new.reshape(nb, t, dm.kvw)


def _layer_spec(layer, shape):
    idx = (layer,) + (0,) * len(shape)
    return pl.BlockSpec((None,) + tuple(shape), lambda *_: idx, pipeline_mode=pl.Buffered(1))


def _mixer_call(dm, prompt, layer, x, tables, wts, caches=None):
    batch, seq, d_model = x.shape
    if prompt:
        nb, t, mchunk = 1, min(PROMPT_TILE, seq), min(MLSTM_CHUNK, seq)
        grid = (batch, seq // t)
        bmap = lambda b, s: (b, s, 0)
        smap = lambda b, s: (b, 0, 0)
        smap4 = lambda b, s: (b, 0, 0, 0)
        tmap = lambda b, s: (s, 0)
        sem = ("arbitrary", "arbitrary")
    else:
        nb, t, mchunk = min(SAMPLE_ROWS, batch), seq, seq
        grid = (batch // nb,)
        bmap = lambda b: (b, 0, 0)
        smap = bmap
        smap4 = lambda b: (b, 0, 0, 0)
        tmap = lambda b: (0, 0)
        sem = ("arbitrary",)
    assert seq % t == 0 and t % mchunk == 0 and t % CHUNK == 0 and (nb * t) % LANES == 0
    assert batch % nb == 0 and mchunk & (mchunk - 1) == 0 and mchunk <= LANES
    win = 2 * CHUNK

    in_specs = [
        pl.BlockSpec((nb, t, d_model), bmap),
        pl.BlockSpec((t, LANES), tmap), pl.BlockSpec((t, LANES), tmap), pl.BlockSpec((t, LANES), tmap),
        _layer_spec(layer, (1, d_model)),
        _layer_spec(layer, (d_model, dm.tm_width)),
        _layer_spec(layer, (dm.fm_rows, d_model)),
        _layer_spec(layer, (2 * SUBLANES, LANES)),
        _layer_spec(layer, (1, dm.aw)),
        _layer_spec(layer, (dm.ah, A_BLOCK, A_BLOCK)),
        _layer_spec(layer, (A_BLOCK, dm.aw)),
        pl.BlockSpec(memory_space=pltpu.SMEM),
        _layer_spec(layer, (d_model, d_model)),
    ]
    args = [x, *tables, wts["attn_norm"], wts["w_tm"], wts["w_fm"], wts["gate_b"], wts["gmlp_norm"],
            wts["gmlp_w"], wts["gmlp_bx"], wts["sinks"], wts["w_out"]]
    state_shapes = [jax.ShapeDtypeStruct((batch, dm.ch, HEAD_DIM, HEAD_DIM), F32),
                    jax.ShapeDtypeStruct((batch, dm.ch, HEAD_DIM), F32),
                    jax.ShapeDtypeStruct((batch, SUBLANES, LANES), F32)]
    state_specs = [pl.BlockSpec((nb, dm.ch, HEAD_DIM, HEAD_DIM), smap4),
                   pl.BlockSpec((nb, dm.ch, HEAD_DIM), smap),
                   pl.BlockSpec((nb, SUBLANES, LANES), smap)]
    if prompt:
        out_shape = [jax.ShapeDtypeStruct((batch, seq, d_model), F32),
                     jax.ShapeDtypeStruct((batch, win, dm.kvw), F32),
                     jax.ShapeDtypeStruct((batch, win, dm.kvw), F32)] + state_shapes
        out_specs = [pl.BlockSpec((nb, t, d_model), bmap),
                     pl.BlockSpec((1, win, dm.kvw), smap),
                     pl.BlockSpec((1, win, dm.kvw), smap)] + state_specs
    else:
        ck, cv, c0, n0, m0 = caches
        lmap = lambda b: (layer, b, 0, 0)
        in_specs += [pl.BlockSpec((None, nb, win, dm.kvw), lmap),
                     pl.BlockSpec((None, nb, win, dm.kvw), lmap),
                     pl.BlockSpec((None, nb, dm.ch, HEAD_DIM, HEAD_DIM), lambda b: (layer, b, 0, 0, 0)),
                     pl.BlockSpec((None, nb, dm.ch, HEAD_DIM), lmap),
                     pl.BlockSpec(memory_space=pltpu.SMEM)]
        args += [ck, cv, c0, n0, m0]
        out_shape = [jax.ShapeDtypeStruct((batch, seq, d_model), F32),
                     jax.ShapeDtypeStruct((batch, seq, dm.aw), F32),
                     jax.ShapeDtypeStruct((batch, seq, dm.kvw), F32),
                     jax.ShapeDtypeStruct((batch, seq, dm.kvw), F32)] + state_shapes
        out_specs = [pl.BlockSpec((nb, t, d_model), bmap),
                     pl.BlockSpec((nb, t, dm.aw), bmap),
                     pl.BlockSpec((nb, t, dm.kvw), bmap),
                     pl.BlockSpec((nb, t, dm.kvw), bmap)] + state_specs
    scratch = [pltpu.VMEM((nb, dm.kv, win + t, LANES), BF16),
               pltpu.VMEM((nb, dm.kv, win + t, 2 * LANES), BF16),
               pltpu.VMEM((nb * t, d_model), F32),
               pltpu.VMEM((dm.cw, nb * t), F32),
               pltpu.VMEM((nb, dm.ch, LANES, LANES), F32)]
    return pl.pallas_call(
        functools.partial(_mixer_kernel, dm, prompt, layer, nb, t, mchunk),
        grid=grid, in_specs=in_specs, out_specs=out_specs, out_shape=out_shape,
        scratch_shapes=scratch,
        compiler_params=pltpu.CompilerParams(dimension_semantics=sem,
                                             vmem_limit_bytes=VMEM_LIMIT_BYTES),
    )(*args)


def _ffn_kernel(prompt, final, nb, t, d_ff, *refs):
    if prompt:
        (x_ref, g_ref, wup_ref, cw_ref, cb_ref, wdn_ref, fin_ref,
         y_ref, cst_ref, hist_ref, gbuf) = refs
    else:
        (x_ref, g_ref, wup_ref, cw_ref, cb_ref, wdn_ref, fin_ref, cc_ref,
         y_ref, cst_ref, hist_ref, gbuf) = refs
    d_model = x_ref.shape[-1]
    rows = nb * t
    hist = CONV_W - 1
    assert hist == 2

    if prompt:
        @pl.when(pl.program_id(1) == 0)
        def _():
            hist_ref[...] = jnp.zeros(hist_ref.shape, F32)
    else:
        for bi in range(nb):
            hist_ref[bi] = jnp.concatenate(
                [jnp.zeros((SUBLANES - hist, 2 * d_ff), F32), cc_ref[bi]], axis=0)

    x = x_ref[...].reshape(rows, d_model)
    xn = _rmsnorm(x, g_ref[...]).astype(BF16)

    row8 = lax.broadcasted_iota(jnp.int32, (SUBLANES, FFN_COLS), 0)
    n_blocks = d_ff // FFN_COLS
    splits = set(range(FFN_DOWN_GROUP, n_blocks, FFN_DOWN_GROUP)) | {n_blocks}
    y = x
    k0 = 0

    def up_pair(j):
        return [_dot(xn, wup_ref[:, half * d_ff + j * FFN_COLS:half * d_ff + (j + 1) * FFN_COLS])
                for half in range(2)]

    ups = up_pair(0)
    for j in range(n_blocks):
        ups_next = up_pair(j + 1) if j + 1 < n_blocks else None
        conv = []
        for half in range(2):
            c0 = half * d_ff + j * FFN_COLS
            cs = slice(c0, c0 + FFN_COLS)
            up = ups[half]
            w0, w1, w2 = cw_ref[0:1, cs], cw_ref[1:2, cs], cw_ref[2:3, cs]
            parts = []
            for bi in range(nb):
                ub = up[bi * t:(bi + 1) * t]
                prev = hist_ref[bi, :, cs]
                r1, r2 = pltpu.roll(ub, 1, 0), pltpu.roll(ub, 2, 0)
                h1 = jnp.where(row8 < 1, pltpu.roll(prev, 1, 0), r1[0:SUBLANES])
                h2 = jnp.where(row8 < 2, pltpu.roll(prev, 2, 0), r2[0:SUBLANES])
                s1 = jnp.concatenate([h1, r1[SUBLANES:]], axis=0)
                s2 = jnp.concatenate([h2, r2[SUBLANES:]], axis=0)
                parts.append(cb_ref[:, cs] + s2 * w0 + s1 * w1 + ub * w2)
                hist_ref[bi, :, cs] = ub[t - SUBLANES:t]
                cst_ref[bi, :, cs] = ub[t - hist:t]
            conv.append(parts[0] if nb == 1 else jnp.concatenate(parts, axis=0))
        gbuf[:, j * FFN_COLS:(j + 1) * FFN_COLS] = (_gelu(conv[0]) * conv[1]).astype(BF16)
        if (j + 1) in splits:
            k1 = (j + 1) * FFN_COLS
            y = y + _dot(gbuf[:, k0:k1], wdn_ref[k0:k1, :])
            k0 = k1
        ups = ups_next
    if final:
        y = _rmsnorm(y, fin_ref[...])
    y_ref[...] = y.reshape(nb, t, d_model)


def _ffn_call(prompt, final, layer, x, wts, final_norm, cache_conv=None):
    batch, seq, d_model = x.shape
    d_ff = wts["w_down"].shape[1]
    if prompt:
        nb, t = 1, min(FFN_TILE, seq)
        grid = (batch, seq // t)
        bmap = lambda b, s: (b, s, 0)
        smap = lambda b, s: (b, 0, 0)
        sem = ("arbitrary", "arbitrary")
    else:
        nb, t = min(SAMPLE_ROWS, batch), seq
        grid = (batch // nb,)
        bmap = lambda b: (b, 0, 0)
        smap = bmap
        sem = ("arbitrary",)
    assert seq % t == 0 and t % SUBLANES == 0 and t > SUBLANES and batch % nb == 0
    assert d_ff % FFN_COLS == 0
    hist = CONV_W - 1
    in_specs = [
        pl.BlockSpec((nb, t, d_model), bmap),
        _layer_spec(layer, (1, d_model)),
        _layer_spec(layer, (d_model, 2 * d_ff)),
        _layer_spec(layer, (CONV_W, 2 * d_ff)),
        _layer_spec(layer, (1, 2 * d_ff)),
        _layer_spec(layer, (d_ff, d_model)),
        _layer_spec(0, (1, d_model)),
    ]
    args = [x, wts["ffn_norm"], wts["w_up"], wts["conv_w"], wts["conv_b"], wts["w_down"],
            final_norm]
    if not prompt:
        in_specs.append(pl.BlockSpec((None, nb, hist, 2 * d_ff), lambda b: (layer, b, 0, 0)))
        args.append(cache_conv)
    out_shape = [jax.ShapeDtypeStruct((batch, seq, d_model), F32),
                 jax.ShapeDtypeStruct((batch, hist, 2 * d_ff), F32)]
    out_specs = [pl.BlockSpec((nb, t, d_model), bmap),
                 pl.BlockSpec((nb, hist, 2 * d_ff), smap)]
    scratch = [pltpu.VMEM((nb, SUBLANES, 2 * d_ff), F32),
               pltpu.VMEM((nb * t, d_ff), BF16)]
    return pl.pallas_call(
        functools.partial(_ffn_kernel, prompt, final, nb, t, d_ff),
        grid=grid, in_specs=in_specs, out_specs=out_specs, out_shape=out_shape,
        scratch_shapes=scratch,
        compiler_params=pltpu.CompilerParams(dimension_semantics=sem,
                                             vmem_limit_bytes=VMEM_LIMIT_BYTES),
    )(*args)


def _rope_tables(pos):
    half = ROT_DIM // 2
    inv_freq = ROPE_THETA ** (-jnp.arange(half, dtype=F32) / half)
    ang = pos.astype(F32)[:, None] * inv_freq[None, :]
    cos, sin = jnp.cos(ang), jnp.sin(ang)
    ones = jnp.ones((pos.shape[0], HEAD_DIM - ROT_DIM), F32)
    zeros = jnp.zeros((pos.shape[0], HEAD_DIM - ROT_DIM), F32)
    zh = jnp.zeros_like(sin)
    cos_t = jnp.concatenate([cos, cos, ones], axis=1)
    sin_a = jnp.concatenate([-sin, zh, zeros], axis=1)
    sin_b = jnp.concatenate([zh, sin, zeros], axis=1)
    rep = LANES // HEAD_DIM
    return tuple(jnp.tile(a, (1, rep)) for a in (cos_t, sin_a, sin_b))


def _prep_weights(dm, attn_norm, w_in, gmlp_norm, gmlp_w, gmlp_b, attn_sinks, mlstm_gate_b,
                  w_out, ffn_norm, w_up, conv_w, conv_b, w_down):
    def cols(off, width):
        return w_in[:, :, off:off + width]

    w_tm = jnp.concatenate([cols(dm.off_a_u, dm.aw), cols(dm.off_a_v, dm.aw), cols(dm.off_b_q, dm.bw),
                            cols(dm.off_b_k, dm.kvw), cols(dm.off_b_v, dm.kvw), cols(dm.off_c_k, dm.cw),
                            cols(dm.off_c_o, dm.cw)], axis=2)
    g_i, g_f = cols(dm.off_c_g, dm.ch), cols(dm.off_c_g + dm.ch, dm.ch)
    w_fm = jnp.swapaxes(
        jnp.concatenate([cols(dm.off_c_q, dm.cw), cols(dm.off_c_v, dm.cw), g_i, g_f, g_f, g_i], axis=2),
        1, 2)
    b_i, b_f = mlstm_gate_b[:, :dm.ch], mlstm_gate_b[:, dm.ch:]
    gate_b = jnp.concatenate([b_i, b_f, b_f, b_i], axis=1)
    depth = w_in.shape[0]
    return {
        "attn_norm": attn_norm[:, None, :],
        "w_tm": w_tm.astype(BF16),
        "w_fm": w_fm.astype(BF16),
        "gate_b": jnp.broadcast_to(gate_b[:, :, None], (depth, 2 * SUBLANES, LANES)),
        "gmlp_norm": gmlp_norm[:, None, :],
        "gmlp_w": gmlp_w,
        "gmlp_bx": jnp.repeat(jnp.swapaxes(gmlp_b, 1, 2), HEAD_DIM, axis=2),
        "sinks": attn_sinks,
        "w_out": w_out.astype(BF16),
        "ffn_norm": ffn_norm[:, None, :],
        "w_up": w_up.astype(BF16),
        "conv_w": conv_w,
        "conv_b": conv_b[:, None, :],
        "w_down": w_down.astype(BF16),
    }


def kernel(x_prompt, x_sample, cache_k, cache_v, state_C, state_n, state_m, cache_conv, attn_norm, w_in, gmlp_norm, gmlp_w, gmlp_b, attn_sinks, mlstm_gate_b, w_out, ffn_norm, w_up, conv_w, conv_b, w_down, final_norm):
    depth = w_in.shape[0]
    d_model = x_prompt.shape[-1]
    kv_heads = cache_k.shape[3]
    dm = _Dims(d_model, kv_heads)
    assert w_in.shape[-1] == dm.in_width
    bp, sp = x_prompt.shape[:2]
    bs, ts = x_sample.shape[:2]

    tab_p = _rope_tables(jnp.arange(sp, dtype=jnp.int32))
    tab_s = _rope_tables(PAST_LEN + jnp.arange(ts, dtype=jnp.int32))
    fin = final_norm[None, None, :]
    wts = _prep_weights(dm, attn_norm, w_in, gmlp_norm, gmlp_w, gmlp_b, attn_sinks, mlstm_gate_b,
                        w_out, ffn_norm, w_up, conv_w, conv_b, w_down)
    caches = (cache_k.reshape(depth, bs, -1, dm.kvw), cache_v.reshape(depth, bs, -1, dm.kvw),
              state_C, state_n, state_m)

    xp, xs = x_prompt, x_sample
    p_st, s_st = [], []
    for l in range(depth):
        last = l == depth - 1
        x1, k_last, v_last, c_p, n_p, mm = _mixer_call(dm, True, l, xp, tab_p, wts)
        xp, conv_p = _ffn_call(True, last, l, x1, wts, fin)
        p_st.append((k_last.reshape(bp, -1, kv_heads, HEAD_DIM), v_last.reshape(bp, -1, kv_heads, HEAD_DIM),
                     c_p, n_p, mm[:, :dm.ch, 0], conv_p))
        x1, a_v, k_new, v_new, c_s, n_s, mm = _mixer_call(dm, False, l, xs, tab_s, wts, caches)
        xs, conv_s = _ffn_call(False, last, l, x1, wts, fin, cache_conv)
        s_st.append((a_v, k_new.reshape(bs, ts, kv_heads, HEAD_DIM), v_new.reshape(bs, ts, kv_heads, HEAD_DIM),
                     c_s, n_s, mm[:, :dm.ch, 0], conv_s))

    def stk(states, i):
        return jnp.stack([s[i] for s in states], axis=0)

    return (xp, xs,
            stk(p_st, 0), stk(p_st, 1), stk(p_st, 2), stk(p_st, 3), stk(p_st, 4), stk(p_st, 5),
            stk(s_st, 0), stk(s_st, 1), stk(s_st, 2), stk(s_st, 3), stk(s_st, 4), stk(s_st, 5), stk(s_st, 6))
```

```python
import functools

import numpy as np
import jax
import jax.numpy as jnp
from jax import lax
from jax.experimental import pallas as pl
from jax.experimental.pallas import tpu as pltpu

HEAD_DIM = 64
CHUNK = 64
A_BLOCK = 128
ROT_DIM = 16
ROPE_THETA = 500000.0
ATTN_SCALE = HEAD_DIM ** -0.5
RMS_EPS = 1e-6
CONV_W = 3
PAST_LEN = 1024

LANES = 128
SUBLANES = 8
BF16_ROWS = 16
VMEM_LIMIT_BYTES = 56 * 1024 * 1024

PROMPT_TILE = 512
FFN_TILE = 1024
FFN_DOWN_GROUP = 6
FFN_DOT_ROWS = 256
MLSTM_CHUNK = 128
SAMPLE_ROWS = 8
FFN_COLS = 256

F32 = jnp.float32
BF16 = jnp.bfloat16


def _dot(a, b):
    return jnp.dot(a, b, preferred_element_type=F32)


def _dot_nt(a, b):
    return lax.dot_general(a, b, (((1,), (1,)), ((), ())), preferred_element_type=F32)


def _rmsnorm(x, g):
    ms = jnp.mean(x * x, axis=-1, keepdims=True)
    return x * lax.rsqrt(ms + RMS_EPS) * g


def _gelu2(x):
    return x * (1.0 + lax.erf(x * np.float32(np.sqrt(0.5))))


def _gelu(x):
    return 0.5 * _gelu2(x)


def _log_sigmoid(x):
    return jnp.minimum(x, 0.0) - jnp.log1p(jnp.exp(-jnp.abs(x)))


class _Dims:
    def __init__(self, d_model, kv_heads):
        self.d = d_model
        self.aw = d_model // 4
        self.ah = self.aw // HEAD_DIM
        self.bw = d_model // 2
        self.bh = self.bw // HEAD_DIM
        self.kv = kv_heads
        self.grp = self.bh // kv_heads
        self.kvw = kv_heads * HEAD_DIM
        self.cw = d_model // 4
        self.ch = self.cw // HEAD_DIM
        self.off_a_u = 0
        self.off_a_v = self.off_a_u + self.aw
        self.off_b_q = self.off_a_v + self.aw
        self.off_b_k = self.off_b_q + self.bw
        self.off_b_v = self.off_b_k + self.kvw
        self.off_c_q = self.off_b_v + self.kvw
        self.off_c_k = self.off_c_q + self.cw
        self.off_c_v = self.off_c_k + self.cw
        self.off_c_o = self.off_c_v + self.cw
        self.off_c_g = self.off_c_o + self.cw
        self.in_width = self.off_c_g + 2 * self.ch
        self.tm_a_u = 0
        self.tm_a_v = self.tm_a_u + self.aw
        self.tm_b_q = self.tm_a_v + self.aw
        self.tm_b_k = self.tm_b_q + self.bw
        self.tm_b_v = self.tm_b_k + self.kvw
        self.tm_c_k = self.tm_b_v + self.kvw
        self.tm_c_o = self.tm_c_k + self.cw
        self.tm_width = self.tm_c_o + self.cw
        self.fm_c_q = 0
        self.fm_c_v = self.fm_c_q + self.cw
        self.fm_g = self.fm_c_v + self.cw
        self.fm_rows = self.fm_g + 2 * SUBLANES
        assert self.kvw == LANES and 2 * self.ch == SUBLANES and self.grp % 2 == 0
        assert self.fm_rows % BF16_ROWS == 0 and self.ch % 2 == 0


def _scan_lanes(x, length, op, fill):
    outs = []
    for j in range(x.shape[1] // LANES):
        xj = x[:, j * LANES:(j + 1) * LANES]
        pos = lax.broadcasted_iota(jnp.int32, xj.shape, 1) & (length - 1)
        s = 1
        while s < length:
            xj = op(xj, jnp.where(pos >= s, pltpu.roll(xj, s, 1), fill))
            s *= 2
        outs.append(xj)
    return outs[0] if len(outs) == 1 else jnp.concatenate(outs, axis=1)


def _rope(xt, cos_t, sin_a, sin_b, nb, t):
    up = pltpu.roll(xt, LANES - ROT_DIM // 2, 1)
    dn = pltpu.roll(xt, ROT_DIM // 2, 1)
    shp = (nb, t, LANES)
    out = (xt.reshape(shp) * cos_t[None] + up.reshape(shp) * sin_a[None]
           + dn.reshape(shp) * sin_b[None])
    return out.reshape(nb * t, LANES)


def _last_lane(rows, length):
    lane = lax.broadcasted_iota(jnp.int32, rows.shape, 1)
    return jnp.max(jnp.where(lane == length - 1, rows, -jnp.inf), axis=-1, keepdims=True)


def _mlstm_pair(k_pairs, q_ts, v_ts, b_rows, r_rows, cm_rows, b_lasts, cm_lasts, r_fulls,
                states, m_prevs, causal_t):
    n_chunks = len(k_pairs)
    L = q_ts[0][0].shape[1]
    zeros = jnp.zeros_like(q_ts[0][0])
    lane = lax.broadcasted_iota(jnp.int32, (LANES, LANES), 1)
    own = [lane < HEAD_DIM, lane >= HEAD_DIM]
    m_cur = list(m_prevs)
    pre = []
    for c in range(n_chunks):
        q_exts = [jnp.concatenate([q_ts[c][0], zeros], axis=0),
                  jnp.concatenate([zeros, q_ts[c][1]], axis=0)]
        s_both = _dot(k_pairs[c], jnp.concatenate(q_exts, axis=1).astype(BF16))
        heads, weighted = [], []
        for i in range(2):
            m_prev = m_cur[i]
            g_row = jnp.maximum(m_prev[:, :L], cm_rows[c][i])
            m_t = g_row + b_rows[c][i]
            p_t = jnp.where(causal_t, jnp.exp(r_fulls[c][i][:, :L] - g_row), 0.0)
            s_t = (s_both[:, i * L:(i + 1) * L] * p_t).astype(BF16)
            v_ext = jnp.concatenate([v_ts[c][i], jnp.ones_like(v_ts[c][i])], axis=0)
            rhs = jnp.concatenate(
                [(q_exts[i] * jnp.exp(m_prev[:, :L] - g_row)).astype(BF16), s_t], axis=0)
            m_new = jnp.maximum(m_prev, cm_lasts[c][i]) + b_lasts[c][i]
            decay = jnp.exp(b_lasts[c][i] + m_prev - m_new)
            weighted.append(v_ext * jnp.exp(r_rows[c][i] + (b_lasts[c][i] - m_new[:, :L])))
            heads.append((v_ext.astype(BF16), rhs, m_t, decay))
            m_cur[i] = m_new
        upd = _dot(jnp.concatenate(weighted, axis=0).astype(BF16), k_pairs[c])
        pre.append((heads, [jnp.where(own[0], upd[:LANES], 0.0), jnp.where(own[1], upd[LANES:], 0.0)]))
    st = list(states)
    st_in = []
    for c in range(n_chunks):
        st_in.append(list(st))
        st = [pre[c][0][i][3] * st[i] + pre[c][1][i] for i in range(2)]
    h_ts = []
    for c in range(n_chunks):
        outs = []
        for i in range(2):
            v_ext, rhs, m_t, _ = pre[c][0][i]
            comb = _dot(jnp.concatenate([st_in[c][i].astype(BF16), v_ext], axis=1), rhs)
            den = comb[HEAD_DIM:HEAD_DIM + 1, :]
            outs.append(comb[:HEAD_DIM, :] / jnp.maximum(jnp.abs(den), jnp.exp(-m_t)))
        h_ts.append(outs)
    return h_ts, st, m_cur


def _mixer_kernel(dm, prompt, layer, nb, t, mchunk, *refs):
    if prompt:
        (x_ref, cos_ref, sa_ref, sb_ref, ng_ref, wtm_ref, wfm_ref, gb_ref, gg_ref, gw_ref, gbx_ref,
         sinks_all, wout_ref,
         x1_ref, klast_ref, vlast_ref, cout_ref, nout_ref, mm_ref,
         kbuf, vbuf, mixed_ref, ht_ref, mst_ref) = refs
    else:
        (x_ref, cos_ref, sa_ref, sb_ref, ng_ref, wtm_ref, wfm_ref, gb_ref, gg_ref, gw_ref, gbx_ref,
         sinks_all, wout_ref, ck_ref, cv_ref, c0_ref, n0_ref, m0_all,
         x1_ref, av_ref, knew_ref, vnew_ref, cout_ref, nout_ref, mm_ref,
         kbuf, vbuf, mixed_ref, ht_ref, mst_ref) = refs
    d_model = dm.d
    rows = nb * t
    win = 2 * CHUNK
    step = pl.program_id(1) if prompt else None

    if prompt:
        @pl.when(step == 0)
        def _():
            kbuf[:, :, 0:win, :] = jnp.zeros((nb, dm.kv, win, LANES), BF16)
            vbuf[:, :, 0:win, :] = jnp.zeros((nb, dm.kv, win, 2 * LANES), BF16)
            mst_ref[...] = jnp.zeros(mst_ref.shape, F32)
            mm_ref[...] = jnp.zeros(mm_ref.shape, F32)

    x = x_ref[...].reshape(rows, d_model)
    xn = _rmsnorm(x, ng_ref[...]).astype(BF16)

    def sec(off, width):
        return _dot(xn, wtm_ref[:, off:off + width])

    cos_t, sin_a, sin_b = cos_ref[...], sa_ref[...], sb_ref[...]

    a_u = _gelu(sec(dm.tm_a_u, dm.aw))
    a_v = _rmsnorm(_gelu(sec(dm.tm_a_v, dm.aw)), gg_ref[...])
    zq = sec(dm.tm_b_q, dm.bw)
    q_tiles = [_rope(zq[:, j * LANES:(j + 1) * LANES], cos_t, sin_a, sin_b, nb, t) * ATTN_SCALE
               for j in range(dm.bw // LANES)]
    k_rot = _rope(sec(dm.tm_b_k, dm.kvw), cos_t, sin_a, sin_b, nb, t)
    v_new = sec(dm.tm_b_v, dm.kvw)
    c_k = (sec(dm.tm_c_k, dm.cw) * ATTN_SCALE).astype(BF16)
    c_o = jax.nn.sigmoid(sec(dm.tm_c_o, dm.cw))
    fm = jnp.concatenate([_dot_nt(wfm_ref[0:dm.fm_c_v, :], xn),
                          _dot_nt(wfm_ref[dm.fm_c_v:dm.fm_rows, :], xn)], axis=0)
    g_if = fm[dm.fm_g:dm.fm_g + SUBLANES, :] + gb_ref[0:SUBLANES, 0:1]
    g_fi = fm[dm.fm_g + SUBLANES:dm.fm_g + 2 * SUBLANES, :] + gb_ref[SUBLANES:2 * SUBLANES, 0:1]
    cum_b = _scan_lanes(_log_sigmoid(g_fi), mchunk, jnp.add, 0.0)
    r_all = g_if - cum_b
    cm_all = _scan_lanes(r_all, mchunk, jnp.maximum, -jnp.inf)

    lane_kv = lax.broadcasted_iota(jnp.int32, (rows, LANES), 1)
    k_sw, v_sw = pltpu.roll(k_rot, HEAD_DIM, 1), pltpu.roll(v_new, HEAD_DIM, 1)
    ones_t = jnp.ones((t, LANES), BF16)

    def dup(a, a_sw, g):
        lo = lane_kv < HEAD_DIM
        return (jnp.where(lo, a, a_sw) if g == 0 else jnp.where(lo, a_sw, a)).astype(BF16)

    if not prompt:
        mm_ref[...] = jnp.zeros(mm_ref.shape, F32)
        lane_c = lax.broadcasted_iota(jnp.int32, (win, LANES), 1) < HEAD_DIM
        ones_w = jnp.ones((win, LANES), BF16)
        for bi in range(nb):
            ck, cv = ck_ref[bi], cv_ref[bi]
            ck_sw, cv_sw = pltpu.roll(ck, HEAD_DIM, 1), pltpu.roll(cv, HEAD_DIM, 1)
            for g in range(dm.kv):
                ka, kb = (ck, ck_sw) if g == 0 else (ck_sw, ck)
                va, vb = (cv, cv_sw) if g == 0 else (cv_sw, cv)
                kbuf[bi, g, 0:win, :] = jnp.where(lane_c, ka, kb).astype(BF16)
                vbuf[bi, g, 0:win, 0:LANES] = jnp.where(lane_c, va, vb).astype(BF16)
                vbuf[bi, g, 0:win, LANES:2 * LANES] = ones_w
    for g in range(dm.kv):
        kd, vd = dup(k_rot, k_sw, g), dup(v_new, v_sw, g)
        for bi in range(nb):
            kbuf[bi, g, win:win + t, :] = kd[bi * t:(bi + 1) * t]
            vbuf[bi, g, win:win + t, 0:LANES] = vd[bi * t:(bi + 1) * t]
            vbuf[bi, g, win:win + t, LANES:2 * LANES] = ones_t

    blk = min(A_BLOCK, t)
    ri = lax.broadcasted_iota(jnp.int32, (A_BLOCK, A_BLOCK), 0)
    ci = lax.broadcasted_iota(jnp.int32, (A_BLOCK, A_BLOCK), 1)
    block_causal = (ci // CHUNK) <= (ri // CHUNK)
    w_cat = jnp.concatenate([jnp.where(block_causal, gw_ref[h], 0.0)[:blk, :blk].astype(BF16)
                             for h in range(dm.ah)], axis=1)
    lane_a = lax.broadcasted_iota(jnp.int32, (blk, dm.aw), 1) // HEAD_DIM
    gbx = gbx_ref[0:blk, :]
    for r0 in range(0, rows, blk):
        vb = a_v[r0:r0 + blk]
        v_cat = jnp.concatenate([jnp.where(lane_a == h, vb, 0.0) for h in range(dm.ah)],
                                axis=0).astype(BF16)
        mixed_ref[r0:r0 + blk, 0:dm.aw] = a_u[r0:r0 + blk] * (_dot(w_cat, v_cat) + gbx)

    lane_q = lax.broadcasted_iota(jnp.int32, (CHUNK, LANES), 1) < HEAD_DIM
    key_lane = lax.broadcasted_iota(jnp.int32, (1, win + CHUNK), 1)
    for bi in range(nb):
        for c in range(t // CHUNK):
            r0 = bi * t + c * CHUNK
            key_ok = None
            if prompt and c < win // CHUNK:
                key_ok = (step * t + c * CHUNK - win + key_lane) >= 0
            for g in range(dm.kv):
                parts = []
                for j in range(dm.grp):
                    tile = q_tiles[(g * dm.grp + j) // 2][r0:r0 + CHUNK]
                    parts.append(jnp.where(lane_q if j % 2 == 0 else ~lane_q, tile, 0.0))
                q4 = jnp.concatenate(parts, axis=0).astype(BF16)
                ks = slice(c * CHUNK, c * CHUNK + win + CHUNK)
                s = _dot_nt(q4, kbuf[bi, g, ks, :])
                if key_ok is not None:
                    s = jnp.where(key_ok, s, -jnp.inf)
                p_parts, mx_parts = [], []
                for j in range(dm.grp):
                    sj = s[j * CHUNK:(j + 1) * CHUNK]
                    mxj = jnp.maximum(jnp.max(sj, axis=-1, keepdims=True),
                                      sinks_all[layer, g * dm.grp + j])
                    p_parts.append(jnp.exp(sj - mxj))
                    mx_parts.append(mxj)
                p = jnp.concatenate(p_parts, axis=0).astype(BF16)
                o2 = _dot(p, vbuf[bi, g, ks, :])
                outs = []
                for j in range(dm.grp):
                    rs = slice(j * CHUNK, (j + 1) * CHUNK)
                    den = o2[rs, LANES:2 * LANES] + jnp.exp(
                        sinks_all[layer, g * dm.grp + j] - mx_parts[j])
                    outs.append(o2[rs, 0:LANES] / den)
                for i in range(dm.grp // 2):
                    col = dm.aw + (g * dm.grp + 2 * i) * HEAD_DIM
                    mixed_ref[r0:r0 + CHUNK, col:col + LANES] = jnp.where(
                        lane_q, outs[2 * i], outs[2 * i + 1])

    si = lax.broadcasted_iota(jnp.int32, (mchunk, mchunk), 0)
    ti = lax.broadcasted_iota(jnp.int32, (mchunk, mchunk), 1)
    causal_t = si <= ti
    n_chunks = rows // mchunk
    b_last = [_last_lane(cum_b[:, c * mchunk:(c + 1) * mchunk], mchunk) for c in range(n_chunks)]
    cm_last = [_last_lane(cm_all[:, c * mchunk:(c + 1) * mchunk], mchunk) for c in range(n_chunks)]
    for hp in range(dm.ch // 2):
        heads = (2 * hp, 2 * hp + 1)
        tile = slice(hp * LANES, (hp + 1) * LANES)
        q_hs = [fm[dm.fm_c_q + h * HEAD_DIM:dm.fm_c_q + (h + 1) * HEAD_DIM, :] for h in heads]
        v_hs = [fm[dm.fm_c_v + h * HEAD_DIM:dm.fm_c_v + (h + 1) * HEAD_DIM, :] for h in heads]
        r_fulls = [jnp.broadcast_to(r_all[h:h + 1, :], (LANES, rows)).T for h in heads]
        for bi in range(nb):
            states, m_prevs = [], []
            for i, h in enumerate(heads):
                if prompt:
                    states.append(mst_ref[bi, h])
                    m_prevs.append(mm_ref[bi, h:h + 1, :])
                else:
                    cn = jnp.concatenate(
                        [c0_ref[bi, h],
                         jnp.broadcast_to(n0_ref[bi, h:h + 1, :], (HEAD_DIM, HEAD_DIM))], axis=0)
                    zero = jnp.zeros_like(cn)
                    states.append(jnp.concatenate([zero, cn] if i == 1 else [cn, zero], axis=1))
                    m_prevs.append(jnp.full(
                        (1, LANES), m0_all[layer, pl.program_id(0) * nb + bi, h], F32))
            cis = [bi * (t // mchunk) + c for c in range(t // mchunk)]
            rss = [slice(ci * mchunk, (ci + 1) * mchunk) for ci in cis]
            h_ts, states, m_prevs = _mlstm_pair(
                [c_k[rs, tile] for rs in rss],
                [[q[:, rs] for q in q_hs] for rs in rss], [[v[:, rs] for v in v_hs] for rs in rss],
                [[cum_b[h:h + 1, rs] for h in heads] for rs in rss],
                [[r_all[h:h + 1, rs] for h in heads] for rs in rss],
                [[cm_all[h:h + 1, rs] for h in heads] for rs in rss],
                [[b_last[ci][h:h + 1] for h in heads] for ci in cis],
                [[cm_last[ci][h:h + 1] for h in heads] for ci in cis],
                [[r[rs] for r in r_fulls] for rs in rss], states, m_prevs, causal_t)
            for c, rs in enumerate(rss):
                for i, h in enumerate(heads):
                    ht_ref[h * HEAD_DIM:(h + 1) * HEAD_DIM, rs] = h_ts[c][i]
            for i, h in enumerate(heads):
                if prompt:
                    mst_ref[bi, h] = states[i]
                mm_ref[bi, h:h + 1, :] = m_prevs[i]
                own = states[i][:, HEAD_DIM:] if i == 1 else states[i][:, :HEAD_DIM]
                cout_ref[bi, h] = own[:HEAD_DIM]
                nout_ref[bi, h:h + 1, :] = own[HEAD_DIM:HEAD_DIM + 1]
    col = dm.aw + dm.bw
    mixed_ref[:, col:col + dm.cw] = c_o * ht_ref[...].T

    x1 = x + _dot(mixed_ref[...].astype(BF16), wout_ref[...])
    x1_ref[...] = x1.reshape(nb, t, d_model)

    if prompt:
        for g in range(dm.kv):
            kbuf[0, g, 0:win, :] = kbuf[0, g, t:t + win, :]
            vbuf[0, g, 0:win, :] = vbuf[0, g, t:t + win, :]
        klast_ref[0] = k_rot[t - win:t]
        vlast_ref[0] = v_new[t - win:t]
    else:
        av_ref[...] = a_v.reshape(nb, t, dm.aw)
        knew_ref[...] = k_rot.reshape(nb, t, dm.kvw)
        vnew_ref[...] = v_new.reshape(nb, t, dm.kvw)


def _layer_spec(layer, shape):
    idx = (layer,) + (0,) * len(shape)
    return pl.BlockSpec((None,) + tuple(shape), lambda *_: idx, pipeline_mode=pl.Buffered(1))


def _mixer_call(dm, prompt, layer, x, tables, wts, caches=None):
    batch, seq, d_model = x.shape
    if prompt:
        nb, t, mchunk = 1, min(PROMPT_TILE, seq), min(MLSTM_CHUNK, seq)
        grid = (batch, seq // t)
        bmap = lambda b, s: (b, s, 0)
        smap = lambda b, s: (b, 0, 0)
        smap4 = lambda b, s: (b, 0, 0, 0)
        tmap = lambda b, s: (s, 0)
        sem = ("arbitrary", "arbitrary")
    else:
        nb, t, mchunk = min(SAMPLE_ROWS, batch), seq, seq
        grid = (batch // nb,)
        bmap = lambda b: (b, 0, 0)
        smap = bmap
        smap4 = lambda b: (b, 0, 0, 0)
        tmap = lambda b: (0, 0)
        sem = ("arbitrary",)
    assert seq % t == 0 and t % mchunk == 0 and t % CHUNK == 0 and (nb * t) % LANES == 0
    assert batch % nb == 0 and mchunk & (mchunk - 1) == 0 and mchunk <= LANES
    win = 2 * CHUNK

    in_specs = [
        pl.BlockSpec((nb, t, d_model), bmap),
        pl.BlockSpec((t, LANES), tmap), pl.BlockSpec((t, LANES), tmap), pl.BlockSpec((t, LANES), tmap),
        _layer_spec(layer, (1, d_model)),
        _layer_spec(layer, (d_model, dm.tm_width)),
        _layer_spec(layer, (dm.fm_rows, d_model)),
        _layer_spec(layer, (2 * SUBLANES, LANES)),
        _layer_spec(layer, (1, dm.aw)),
        _layer_spec(layer, (dm.ah, A_BLOCK, A_BLOCK)),
        _layer_spec(layer, (A_BLOCK, dm.aw)),
        pl.BlockSpec(memory_space=pltpu.SMEM),
        _layer_spec(layer, (d_model, d_model)),
    ]
    args = [x, *tables, wts["attn_norm"], wts["w_tm"], wts["w_fm"], wts["gate_b"], wts["gmlp_norm"],
            wts["gmlp_w"], wts["gmlp_bx"], wts["sinks"], wts["w_out"]]
    state_shapes = [jax.ShapeDtypeStruct((batch, dm.ch, HEAD_DIM, HEAD_DIM), F32),
                    jax.ShapeDtypeStruct((batch, dm.ch, HEAD_DIM), F32),
                    jax.ShapeDtypeStruct((batch, SUBLANES, LANES), F32)]
    state_specs = [pl.BlockSpec((nb, dm.ch, HEAD_DIM, HEAD_DIM), smap4),
                   pl.BlockSpec((nb, dm.ch, HEAD_DIM), smap),
                   pl.BlockSpec((nb, SUBLANES, LANES), smap)]
    if prompt:
        out_shape = [jax.ShapeDtypeStruct((batch, seq, d_model), F32),
                     jax.ShapeDtypeStruct((batch, win, dm.kvw), F32),
                     jax.ShapeDtypeStruct((batch, win, dm.kvw), F32)] + state_shapes
        out_specs = [pl.BlockSpec((nb, t, d_model), bmap),
                     pl.BlockSpec((1, win, dm.kvw), smap),
                     pl.BlockSpec((1, win, dm.kvw), smap)] + state_specs
    else:
        ck, cv, c0, n0, m0 = caches
        lmap = lambda b: (layer, b, 0, 0)
        in_specs += [pl.BlockSpec((None, nb, win, dm.kvw), lmap),
                     pl.BlockSpec((None, nb, win, dm.kvw), lmap),
                     pl.BlockSpec((None, nb, dm.ch, HEAD_DIM, HEAD_DIM), lambda b: (layer, b, 0, 0, 0)),
                     pl.BlockSpec((None, nb, dm.ch, HEAD_DIM), lmap),
                     pl.BlockSpec(memory_space=pltpu.SMEM)]
        args += [ck, cv, c0, n0, m0]
        out_shape = [jax.ShapeDtypeStruct((batch, seq, d_model), F32),
                     jax.ShapeDtypeStruct((batch, seq, dm.aw), F32),
                     jax.ShapeDtypeStruct((batch, seq, dm.kvw), F32),
                     jax.ShapeDtypeStruct((batch, seq, dm.kvw), F32)] + state_shapes
        out_specs = [pl.BlockSpec((nb, t, d_model), bmap),
                     pl.BlockSpec((nb, t, dm.aw), bmap),
                     pl.BlockSpec((nb, t, dm.kvw), bmap),
                     pl.BlockSpec((nb, t, dm.kvw), bmap)] + state_specs
    scratch = [pltpu.VMEM((nb, dm.kv, win + t, LANES), BF16),
               pltpu.VMEM((nb, dm.kv, win + t, 2 * LANES), BF16),
               pltpu.VMEM((nb * t, d_model), F32),
               pltpu.VMEM((dm.cw, nb * t), F32),
               pltpu.VMEM((nb, dm.ch, LANES, LANES), F32)]
    return pl.pallas_call(
        functools.partial(_mixer_kernel, dm, prompt, layer, nb, t, mchunk),
        grid=grid, in_specs=in_specs, out_specs=out_specs, out_shape=out_shape,
        scratch_shapes=scratch,
        compiler_params=pltpu.CompilerParams(dimension_semantics=sem,
                                             vmem_limit_bytes=VMEM_LIMIT_BYTES),
    )(*args)


def _ffn_kernel(prompt, final, nb, t, d_ff, *refs):
    if prompt:
        (x_ref, g_ref, wup_ref, cw_ref, cb_ref, wdn_ref, fin_ref,
         y_ref, cst_ref, hist_ref, gbuf) = refs
    else:
        (x_ref, g_ref, wup_ref, cw_ref, cb_ref, wdn_ref, fin_ref, cc_ref,
         y_ref, cst_ref, hist_ref, gbuf) = refs
    d_model = x_ref.shape[-1]
    rows = nb * t
    hist = CONV_W - 1
    assert hist == 2

    if prompt:
        @pl.when(pl.program_id(1) == 0)
        def _():
            hist_ref[...] = jnp.zeros(hist_ref.shape, F32)
    else:
        for bi in range(nb):
            hist_ref[bi] = jnp.concatenate(
                [jnp.zeros((SUBLANES - hist, 2 * d_ff), F32), cc_ref[bi]], axis=0)

    row8 = lax.broadcasted_iota(jnp.int32, (SUBLANES, FFN_COLS), 0)
    n_blocks = d_ff // FFN_COLS
    splits = set(range(FFN_DOWN_GROUP, n_blocks, FFN_DOWN_GROUP)) | {n_blocks}

    x = x_ref[...].reshape(rows, d_model)
    xn = _rmsnorm(x, g_ref[...]).astype(BF16)
    y = x
    k0 = 0
    dot_rows = min(FFN_DOT_ROWS, rows)

    def up_pair(j):
        outs = []
        for half in range(2):
            w = wup_ref[:, half * d_ff + j * FFN_COLS:half * d_ff + (j + 1) * FFN_COLS]
            outs.append(jnp.concatenate(
                [_dot(xn[r0:r0 + dot_rows], w) for r0 in range(0, rows, dot_rows)], axis=0))
        return outs

    for j in range(n_blocks):
        up_j = up_pair(j)
        conv = []
        for half in range(2):
            c0 = half * d_ff + j * FFN_COLS
            cs = slice(c0, c0 + FFN_COLS)
            up = up_j[half]
            w0, w1, w2 = cw_ref[0:1, cs], cw_ref[1:2, cs], cw_ref[2:3, cs]
            parts = []
            for bi in range(nb):
                ub = up[bi * t:(bi + 1) * t]
                prev = hist_ref[bi, :, cs]
                r1, r2 = pltpu.roll(ub, 1, 0), pltpu.roll(ub, 2, 0)
                h1 = jnp.where(row8 < 1, pltpu.roll(prev, 1, 0), r1[0:SUBLANES])
                h2 = jnp.where(row8 < 2, pltpu.roll(prev, 2, 0), r2[0:SUBLANES])
                s1 = jnp.concatenate([h1, r1[SUBLANES:]], axis=0)
                s2 = jnp.concatenate([h2, r2[SUBLANES:]], axis=0)
                parts.append(cb_ref[:, cs] + s2 * w0 + s1 * w1 + ub * w2)
                hist_ref[bi, :, cs] = ub[t - SUBLANES:t]
                cst_ref[bi, :, cs] = ub[t - hist:t]
            conv.append(parts[0] if nb == 1 else jnp.concatenate(parts, axis=0))
        gbuf[:, j * FFN_COLS:(j + 1) * FFN_COLS] = (_gelu2(conv[0]) * conv[1]).astype(BF16)
        if (j + 1) in splits:
            k1 = (j + 1) * FFN_COLS
            y = y + _dot(gbuf[:, k0:k1], wdn_ref[k0:k1, :])
            k0 = k1
    if final:
        y = _rmsnorm(y, fin_ref[...])
    y_ref[...] = y.reshape(nb, t, d_model)


def _ffn_call(prompt, final, layer, x, wts, final_norm, cache_conv=None):
    batch, seq, d_model = x.shape
    d_ff = wts["w_down"].shape[1]
    if prompt:
        nb, t = 1, min(FFN_TILE, seq)
        grid = (batch, seq // t)
        bmap = lambda b, s: (b, s, 0)
        smap = lambda b, s: (b, 0, 0)
        sem = ("arbitrary", "arbitrary")
    else:
        nb, t = min(SAMPLE_ROWS, batch), seq
        grid = (batch // nb,)
        bmap = lambda b: (b, 0, 0)
        smap = bmap
        sem = ("arbitrary",)
    assert seq % t == 0 and t % SUBLANES == 0 and t > SUBLANES and batch % nb == 0
    assert d_ff % FFN_COLS == 0
    hist = CONV_W - 1
    in_specs = [
        pl.BlockSpec((nb, t, d_model), bmap),
        _layer_spec(layer, (1, d_model)),
        _layer_spec(layer, (d_model, 2 * d_ff)),
        _layer_spec(layer, (CONV_W, 2 * d_ff)),
        _layer_spec(layer, (1, 2 * d_ff)),
        _layer_spec(layer, (d_ff, d_model)),
        _layer_spec(0, (1, d_model)),
    ]
    args = [x, wts["ffn_norm"], wts["w_up"], wts["conv_w"], wts["conv_b"], wts["w_down"],
            final_norm]
    if not prompt:
        in_specs.append(pl.BlockSpec((None, nb, hist, 2 * d_ff), lambda b: (layer, b, 0, 0)))
        args.append(cache_conv)
    out_shape = [jax.ShapeDtypeStruct((batch, seq, d_model), F32),
                 jax.ShapeDtypeStruct((batch, hist, 2 * d_ff), F32)]
    out_specs = [pl.BlockSpec((nb, t, d_model), bmap),
                 pl.BlockSpec((nb, hist, 2 * d_ff), smap)]
    scratch = [pltpu.VMEM((nb, SUBLANES, 2 * d_ff), F32),
               pltpu.VMEM((nb * t, d_ff), BF16)]
    return pl.pallas_call(
        functools.partial(_ffn_kernel, prompt, final, nb, t, d_ff),
        grid=grid, in_specs=in_specs, out_specs=out_specs, out_shape=out_shape,
        scratch_shapes=scratch,
        compiler_params=pltpu.CompilerParams(dimension_semantics=sem,
                                             vmem_limit_bytes=VMEM_LIMIT_BYTES),
    )(*args)


def _rope_tables(pos):
    half = ROT_DIM // 2
    inv_freq = ROPE_THETA ** (-jnp.arange(half, dtype=F32) / half)
    ang = pos.astype(F32)[:, None] * inv_freq[None, :]
    cos, sin = jnp.cos(ang), jnp.sin(ang)
    ones = jnp.ones((pos.shape[0], HEAD_DIM - ROT_DIM), F32)
    zeros = jnp.zeros((pos.shape[0], HEAD_DIM - ROT_DIM), F32)
    zh = jnp.zeros_like(sin)
    cos_t = jnp.concatenate([cos, cos, ones], axis=1)
    sin_a = jnp.concatenate([-sin, zh, zeros], axis=1)
    sin_b = jnp.concatenate([zh, sin, zeros], axis=1)
    rep = LANES // HEAD_DIM
    return tuple(jnp.tile(a, (1, rep)) for a in (cos_t, sin_a, sin_b))


def _prep_weights(dm, attn_norm, w_in, gmlp_norm, gmlp_w, gmlp_b, attn_sinks, mlstm_gate_b,
                  w_out, ffn_norm, w_up, conv_w, conv_b, w_down):
    def cols(off, width):
        return w_in[:, :, off:off + width]

    w_tm = jnp.concatenate([cols(dm.off_a_u, dm.aw), cols(dm.off_a_v, dm.aw), cols(dm.off_b_q, dm.bw),
                            cols(dm.off_b_k, dm.kvw), cols(dm.off_b_v, dm.kvw), cols(dm.off_c_k, dm.cw),
                            cols(dm.off_c_o, dm.cw)], axis=2)
    g_i, g_f = cols(dm.off_c_g, dm.ch), cols(dm.off_c_g + dm.ch, dm.ch)
    w_fm = jnp.swapaxes(
        jnp.concatenate([cols(dm.off_c_q, dm.cw), cols(dm.off_c_v, dm.cw), g_i, g_f, g_f, g_i], axis=2),
        1, 2)
    b_i, b_f = mlstm_gate_b[:, :dm.ch], mlstm_gate_b[:, dm.ch:]
    gate_b = jnp.concatenate([b_i, b_f, b_f, b_i], axis=1)
    depth = w_in.shape[0]
    return {
        "attn_norm": attn_norm[:, None, :],
        "w_tm": w_tm.astype(BF16),
        "w_fm": w_fm.astype(BF16),
        "gate_b": jnp.broadcast_to(gate_b[:, :, None], (depth, 2 * SUBLANES, LANES)),
        "gmlp_norm": gmlp_norm[:, None, :],
        "gmlp_w": gmlp_w,
        "gmlp_bx": jnp.repeat(jnp.swapaxes(gmlp_b, 1, 2), HEAD_DIM, axis=2),
        "sinks": attn_sinks,
        "w_out": w_out.astype(BF16),
        "ffn_norm": ffn_norm[:, None, :],
        "w_up": w_up.astype(BF16),
        "conv_w": conv_w,
        "conv_b": conv_b[:, None, :],
        "w_down": (0.5 * w_down).astype(BF16),
    }


def kernel(x_prompt, x_sample, cache_k, cache_v, state_C, state_n, state_m, cache_conv, attn_norm, w_in, gmlp_norm, gmlp_w, gmlp_b, attn_sinks, mlstm_gate_b, w_out, ffn_norm, w_up, conv_w, conv_b, w_down, final_norm):
    depth = w_in.shape[0]
    d_model = x_prompt.shape[-1]
    kv_heads = cache_k.shape[3]
    dm = _Dims(d_model, kv_heads)
    assert w_in.shape[-1] == dm.in_width
    bp, sp = x_prompt.shape[:2]
    bs, ts = x_sample.shape[:2]

    tab_p = _rope_tables(jnp.arange(sp, dtype=jnp.int32))
    tab_s = _rope_tables(PAST_LEN + jnp.arange(ts, dtype=jnp.int32))
    fin = final_norm[None, None, :]
    wts = _prep_weights(dm, attn_norm, w_in, gmlp_norm, gmlp_w, gmlp_b, attn_sinks, mlstm_gate_b,
                        w_out, ffn_norm, w_up, conv_w, conv_b, w_down)
    caches = (cache_k.reshape(depth, bs, -1, dm.kvw), cache_v.reshape(depth, bs, -1, dm.kvw),
              state_C, state_n, state_m)

    xp, xs = x_prompt, x_sample
    p_st, s_st = [], []
    for l in range(depth):
        last = l == depth - 1
        x1, k_last, v_last, c_p, n_p, mm = _mixer_call(dm, True, l, xp, tab_p, wts)
        xp, conv_p = _ffn_call(True, last, l, x1, wts, fin)
        p_st.append((k_last.reshape(bp, -1, kv_heads, HEAD_DIM), v_last.reshape(bp, -1, kv_heads, HEAD_DIM),
                     c_p, n_p, mm[:, :dm.ch, 0], conv_p))
        x1, a_v, k_new, v_new, c_s, n_s, mm = _mixer_call(dm, False, l, xs, tab_s, wts, caches)
        xs, conv_s = _ffn_call(False, last, l, x1, wts, fin, cache_conv)
        s_st.append((a_v, k_new.reshape(bs, ts, kv_heads, HEAD_DIM), v_new.reshape(bs, ts, kv_heads, HEAD_DIM),
                     c_s, n_s, mm[:, :dm.ch, 0], conv_s))

    def stk(states, i):
        return jnp.stack([s[i] for s in states], axis=0)

    return (xp, xs,
            stk(p_st, 0), stk(p_st, 1), stk(p_st, 2), stk(p_st, 3), stk(p_st, 4), stk(p_st, 5),
            stk(s_st, 0), stk(s_st, 1), stk(s_st, 2), stk(s_st, 3), stk(s_st, 4), stk(s_st, 5), stk(s_st, 6))
```

```python
import functools

import numpy as np
import jax
import jax.numpy as jnp
from jax import lax
from jax.experimental import pallas as pl
from jax.experimental.pallas import tpu as pltpu

HEAD_DIM = 64
CHUNK = 64
A_BLOCK = 128
ROT_DIM = 16
ROPE_THETA = 500000.0
ATTN_SCALE = HEAD_DIM ** -0.5
RMS_EPS = 1e-6
CONV_W = 3
PAST_LEN = 1024

LANES = 128
SUBLANES = 8
BF16_ROWS = 16
VMEM_LIMIT_BYTES = 56 * 1024 * 1024

PROMPT_TILE = 512
FFN_TILE = 1024
FFN_DOWN_GROUP = 6
FFN_DOT_ROWS = 256
MLSTM_CHUNK = 128
SAMPLE_ROWS = 8
FFN_COLS = 256

F32 = jnp.float32
BF16 = jnp.bfloat16


def _dot(a, b):
    return jnp.dot(a, b, preferred_element_type=F32)


def _dot_nt(a, b):
    return lax.dot_general(a, b, (((1,), (1,)), ((), ())), preferred_element_type=F32)


def _rmsnorm(x, g):
    ms = jnp.mean(x * x, axis=-1, keepdims=True)
    return x * lax.rsqrt(ms + RMS_EPS) * g


def _gelu2(x):
    return x * (1.0 + lax.erf(x * np.float32(np.sqrt(0.5)).astype(x.dtype)))


def _gelu(x):
    return 0.5 * _gelu2(x)


def _log_sigmoid(x):
    return jnp.minimum(x, 0.0) - jnp.log1p(jnp.exp(-jnp.abs(x)))


class _Dims:
    def __init__(self, d_model, kv_heads):
        self.d = d_model
        self.aw = d_model // 4
        self.ah = self.aw // HEAD_DIM
        self.bw = d_model // 2
        self.bh = self.bw // HEAD_DIM
        self.kv = kv_heads
        self.grp = self.bh // kv_heads
        self.kvw = kv_heads * HEAD_DIM
        self.cw = d_model // 4
        self.ch = self.cw // HEAD_DIM
        self.off_a_u = 0
        self.off_a_v = self.off_a_u + self.aw
        self.off_b_q = self.off_a_v + self.aw
        self.off_b_k = self.off_b_q + self.bw
        self.off_b_v = self.off_b_k + self.kvw
        self.off_c_q = self.off_b_v + self.kvw
        self.off_c_k = self.off_c_q + self.cw
        self.off_c_v = self.off_c_k + self.cw
        self.off_c_o = self.off_c_v + self.cw
        self.off_c_g = self.off_c_o + self.cw
        self.in_width = self.off_c_g + 2 * self.ch
        self.tm_a_u = 0
        self.tm_a_v = self.tm_a_u + self.aw
        self.tm_b_q = self.tm_a_v + self.aw
        self.tm_b_k = self.tm_b_q + self.bw
        self.tm_b_v = self.tm_b_k + self.kvw
        self.tm_c_k = self.tm_b_v + self.kvw
        self.tm_c_o = self.tm_c_k + self.cw
        self.tm_width = self.tm_c_o + self.cw
        self.fm_c_q = 0
        self.fm_c_v = self.fm_c_q + self.cw
        self.fm_g = self.fm_c_v + self.cw
        self.fm_rows = self.fm_g + 2 * SUBLANES
        assert self.kvw == LANES and 2 * self.ch == SUBLANES and self.grp % 2 == 0
        assert self.fm_rows % BF16_ROWS == 0 and self.ch % 2 == 0


def _scan_lanes(x, length, op, fill):
    outs = []
    for j in range(x.shape[1] // LANES):
        xj = x[:, j * LANES:(j + 1) * LANES]
        pos = lax.broadcasted_iota(jnp.int32, xj.shape, 1) & (length - 1)
        s = 1
        while s < length:
            xj = op(xj, jnp.where(pos >= s, pltpu.roll(xj, s, 1), fill))
            s *= 2
        outs.append(xj)
    return outs[0] if len(outs) == 1 else jnp.concatenate(outs, axis=1)


def _rope(xt, cos_t, sin_a, sin_b, nb, t):
    up = pltpu.roll(xt, LANES - ROT_DIM // 2, 1)
    dn = pltpu.roll(xt, ROT_DIM // 2, 1)
    shp = (nb, t, LANES)
    out = (xt.reshape(shp) * cos_t[None] + up.reshape(shp) * sin_a[None]
           + dn.reshape(shp) * sin_b[None])
    return out.reshape(nb * t, LANES)


def _last_lane(rows, length):
    lane = lax.broadcasted_iota(jnp.int32, rows.shape, 1)
    return jnp.max(jnp.where(lane == length - 1, rows, -jnp.inf), axis=-1, keepdims=True)


def _mlstm_pair(k_pairs, q_ts, v_ts, b_rows, r_rows, cm_rows, b_lasts, cm_lasts, r_fulls,
                states, m_prevs, causal_t):
    n_chunks = len(k_pairs)
    L = q_ts[0][0].shape[1]
    zeros = jnp.zeros_like(q_ts[0][0])
    lane = lax.broadcasted_iota(jnp.int32, (LANES, LANES), 1)
    own = [lane < HEAD_DIM, lane >= HEAD_DIM]
    m_cur = list(m_prevs)
    pre = []
    for c in range(n_chunks):
        q_exts = [jnp.concatenate([q_ts[c][0], zeros], axis=0),
                  jnp.concatenate([zeros, q_ts[c][1]], axis=0)]
        s_both = _dot(k_pairs[c], jnp.concatenate(q_exts, axis=1).astype(BF16))
        heads, weighted = [], []
        for i in range(2):
            m_prev = m_cur[i]
            g_row = jnp.maximum(m_prev[:, :L], cm_rows[c][i])
            m_t = g_row + b_rows[c][i]
            p_t = jnp.where(causal_t, jnp.exp(r_fulls[c][i][:, :L] - g_row), 0.0)
            s_t = (s_both[:, i * L:(i + 1) * L] * p_t).astype(BF16)
            v_ext = jnp.concatenate([v_ts[c][i], jnp.ones_like(v_ts[c][i])], axis=0)
            rhs = jnp.concatenate(
                [(q_exts[i] * jnp.exp(m_prev[:, :L] - g_row)).astype(BF16), s_t], axis=0)
            m_new = jnp.maximum(m_prev, cm_lasts[c][i]) + b_lasts[c][i]
            decay = jnp.exp(b_lasts[c][i] + m_prev - m_new)
            weighted.append(v_ext * jnp.exp(r_rows[c][i] + (b_lasts[c][i] - m_new[:, :L])))
            heads.append((v_ext.astype(BF16), rhs, m_t, decay))
            m_cur[i] = m_new
        upd = _dot(jnp.concatenate(weighted, axis=0).astype(BF16), k_pairs[c])
        pre.append((heads, [jnp.where(own[0], upd[:LANES], 0.0), jnp.where(own[1], upd[LANES:], 0.0)]))
    st = list(states)
    st_in = []
    for c in range(n_chunks):
        st_in.append(list(st))
        st = [pre[c][0][i][3] * st[i] + pre[c][1][i] for i in range(2)]
    h_ts = []
    for c in range(n_chunks):
        outs = []
        for i in range(2):
            v_ext, rhs, m_t, _ = pre[c][0][i]
            comb = _dot(jnp.concatenate([st_in[c][i].astype(BF16), v_ext], axis=1), rhs)
            den = comb[HEAD_DIM:HEAD_DIM + 1, :]
            outs.append(comb[:HEAD_DIM, :] / jnp.maximum(jnp.abs(den), jnp.exp(-m_t)))
        h_ts.append(outs)
    return h_ts, st, m_cur


def _mixer_kernel(dm, prompt, layer, nb, t, mchunk, *refs):
    if prompt:
        (x_ref, cos_ref, sa_ref, sb_ref, ng_ref, wtm_ref, wfm_ref, gb_ref, gg_ref, gw_ref, gbx_ref,
         sinks_all, wout_ref,
         x1_ref, klast_ref, vlast_ref, cout_ref, nout_ref, mm_ref,
         kbuf, vbuf, mixed_ref, ht_ref, mst_ref) = refs
    else:
        (x_ref, cos_ref, sa_ref, sb_ref, ng_ref, wtm_ref, wfm_ref, gb_ref, gg_ref, gw_ref, gbx_ref,
         sinks_all, wout_ref, ck_ref, cv_ref, c0_ref, n0_ref, m0_all,
         x1_ref, av_ref, knew_ref, vnew_ref, cout_ref, nout_ref, mm_ref,
         kbuf, vbuf, mixed_ref, ht_ref, mst_ref) = refs
    d_model = dm.d
    rows = nb * t
    win = 2 * CHUNK
    step = pl.program_id(1) if prompt else None

    if prompt:
        @pl.when(step == 0)
        def _():
            kbuf[:, :, 0:win, :] = jnp.zeros((nb, dm.kv, win, LANES), BF16)
            vbuf[:, :, 0:win, :] = jnp.zeros((nb, dm.kv, win, 2 * LANES), BF16)
            mst_ref[...] = jnp.zeros(mst_ref.shape, F32)
            mm_ref[...] = jnp.zeros(mm_ref.shape, F32)

    x = x_ref[...].reshape(rows, d_model)
    xn = _rmsnorm(x, ng_ref[...]).astype(BF16)

    def sec(off, width):
        return _dot(xn, wtm_ref[:, off:off + width])

    cos_t, sin_a, sin_b = cos_ref[...], sa_ref[...], sb_ref[...]

    a_u = _gelu(sec(dm.tm_a_u, dm.aw))
    a_v = _rmsnorm(_gelu(sec(dm.tm_a_v, dm.aw)), gg_ref[...])
    zq = sec(dm.tm_b_q, dm.bw)
    q_tiles = [_rope(zq[:, j * LANES:(j + 1) * LANES], cos_t, sin_a, sin_b, nb, t) * ATTN_SCALE
               for j in range(dm.bw // LANES)]
    k_rot = _rope(sec(dm.tm_b_k, dm.kvw), cos_t, sin_a, sin_b, nb, t)
    v_new = sec(dm.tm_b_v, dm.kvw)
    c_k = (sec(dm.tm_c_k, dm.cw) * ATTN_SCALE).astype(BF16)
    c_o = jax.nn.sigmoid(sec(dm.tm_c_o, dm.cw))
    fm = jnp.concatenate([_dot_nt(wfm_ref[0:dm.fm_c_v, :], xn),
                          _dot_nt(wfm_ref[dm.fm_c_v:dm.fm_rows, :], xn)], axis=0)
    g_if = fm[dm.fm_g:dm.fm_g + SUBLANES, :] + gb_ref[0:SUBLANES, 0:1]
    g_fi = fm[dm.fm_g + SUBLANES:dm.fm_g + 2 * SUBLANES, :] + gb_ref[SUBLANES:2 * SUBLANES, 0:1]
    cum_b = _scan_lanes(_log_sigmoid(g_fi), mchunk, jnp.add, 0.0)
    r_all = g_if - cum_b
    cm_all = _scan_lanes(r_all, mchunk, jnp.maximum, -jnp.inf)

    lane_kv = lax.broadcasted_iota(jnp.int32, (rows, LANES), 1)
    k_sw, v_sw = pltpu.roll(k_rot, HEAD_DIM, 1), pltpu.roll(v_new, HEAD_DIM, 1)
    ones_t = jnp.ones((t, LANES), BF16)

    def dup(a, a_sw, g):
        lo = lane_kv < HEAD_DIM
        return (jnp.where(lo, a, a_sw) if g == 0 else jnp.where(lo, a_sw, a)).astype(BF16)

    if not prompt:
        mm_ref[...] = jnp.zeros(mm_ref.shape, F32)
        lane_c = lax.broadcasted_iota(jnp.int32, (win, LANES), 1) < HEAD_DIM
        ones_w = jnp.ones((win, LANES), BF16)
        for bi in range(nb):
            ck, cv = ck_ref[bi], cv_ref[bi]
            ck_sw, cv_sw = pltpu.roll(ck, HEAD_DIM, 1), pltpu.roll(cv, HEAD_DIM, 1)
            for g in range(dm.kv):
                ka, kb = (ck, ck_sw) if g == 0 else (ck_sw, ck)
                va, vb = (cv, cv_sw) if g == 0 else (cv_sw, cv)
                kbuf[bi, g, 0:win, :] = jnp.where(lane_c, ka, kb).astype(BF16)
                vbuf[bi, g, 0:win, 0:LANES] = jnp.where(lane_c, va, vb).astype(BF16)
                vbuf[bi, g, 0:win, LANES:2 * LANES] = ones_w
    for g in range(dm.kv):
        kd, vd = dup(k_rot, k_sw, g), dup(v_new, v_sw, g)
        for bi in range(nb):
            kbuf[bi, g, win:win + t, :] = kd[bi * t:(bi + 1) * t]
            vbuf[bi, g, win:win + t, 0:LANES] = vd[bi * t:(bi + 1) * t]
            vbuf[bi, g, win:win + t, LANES:2 * LANES] = ones_t

    blk = min(A_BLOCK, t)
    ri = lax.broadcasted_iota(jnp.int32, (A_BLOCK, A_BLOCK), 0)
    ci = lax.broadcasted_iota(jnp.int32, (A_BLOCK, A_BLOCK), 1)
    block_causal = (ci // CHUNK) <= (ri // CHUNK)
    w_cat = jnp.concatenate([jnp.where(block_causal, gw_ref[h], 0.0)[:blk, :blk].astype(BF16)
                             for h in range(dm.ah)], axis=1)
    lane_a = lax.broadcasted_iota(jnp.int32, (blk, dm.aw), 1) // HEAD_DIM
    gbx = gbx_ref[0:blk, :]
    for r0 in range(0, rows, blk):
        vb = a_v[r0:r0 + blk]
        v_cat = jnp.concatenate([jnp.where(lane_a == h, vb, 0.0) for h in range(dm.ah)],
                                axis=0).astype(BF16)
        mixed_ref[r0:r0 + blk, 0:dm.aw] = a_u[r0:r0 + blk] * (_dot(w_cat, v_cat) + gbx)

    lane_q = lax.broadcasted_iota(jnp.int32, (CHUNK, LANES), 1) < HEAD_DIM
    key_lane = lax.broadcasted_iota(jnp.int32, (1, win + CHUNK), 1)
    for bi in range(nb):
        for c in range(t // CHUNK):
            r0 = bi * t + c * CHUNK
            key_ok = None
            if prompt and c < win // CHUNK:
                key_ok = (step * t + c * CHUNK - win + key_lane) >= 0
            for g in range(dm.kv):
                parts = []
                for j in range(dm.grp):
                    tile = q_tiles[(g * dm.grp + j) // 2][r0:r0 + CHUNK]
                    parts.append(jnp.where(lane_q if j % 2 == 0 else ~lane_q, tile, 0.0))
                q4 = jnp.concatenate(parts, axis=0).astype(BF16)
                ks = slice(c * CHUNK, c * CHUNK + win + CHUNK)
                s = _dot_nt(q4, kbuf[bi, g, ks, :])
                if key_ok is not None:
                    s = jnp.where(key_ok, s, -jnp.inf)
                p_parts, mx_parts = [], []
                for j in range(dm.grp):
                    sj = s[j * CHUNK:(j + 1) * CHUNK]
                    mxj = jnp.maximum(jnp.max(sj, axis=-1, keepdims=True),
                                      sinks_all[layer, g * dm.grp + j])
                    p_parts.append(jnp.exp(sj - mxj))
                    mx_parts.append(mxj)
                p = jnp.concatenate(p_parts, axis=0).astype(BF16)
                o2 = _dot(p, vbuf[bi, g, ks, :])
                outs = []
                for j in range(dm.grp):
                    rs = slice(j * CHUNK, (j + 1) * CHUNK)
                    den = o2[rs, LANES:2 * LANES] + jnp.exp(
                        sinks_all[layer, g * dm.grp + j] - mx_parts[j])
                    outs.append(o2[rs, 0:LANES] / den)
                for i in range(dm.grp // 2):
                    col = dm.aw + (g * dm.grp + 2 * i) * HEAD_DIM
                    mixed_ref[r0:r0 + CHUNK, col:col + LANES] = jnp.where(
                        lane_q, outs[2 * i], outs[2 * i + 1])

    si = lax.broadcasted_iota(jnp.int32, (mchunk, mchunk), 0)
    ti = lax.broadcasted_iota(jnp.int32, (mchunk, mchunk), 1)
    causal_t = si <= ti
    n_chunks = rows // mchunk
    b_last = [_last_lane(cum_b[:, c * mchunk:(c + 1) * mchunk], mchunk) for c in range(n_chunks)]
    cm_last = [_last_lane(cm_all[:, c * mchunk:(c + 1) * mchunk], mchunk) for c in range(n_chunks)]
    for hp in range(dm.ch // 2):
        heads = (2 * hp, 2 * hp + 1)
        tile = slice(hp * LANES, (hp + 1) * LANES)
        q_hs = [fm[dm.fm_c_q + h * HEAD_DIM:dm.fm_c_q + (h + 1) * HEAD_DIM, :] for h in heads]
        v_hs = [fm[dm.fm_c_v + h * HEAD_DIM:dm.fm_c_v + (h + 1) * HEAD_DIM, :] for h in heads]
        r_fulls = [jnp.broadcast_to(r_all[h:h + 1, :], (LANES, rows)).T for h in heads]
        for bi in range(nb):
            states, m_prevs = [], []
            for i, h in enumerate(heads):
                if prompt:
                    states.append(mst_ref[bi, h])
                    m_prevs.append(mm_ref[bi, h:h + 1, :])
                else:
                    cn = jnp.concatenate(
                        [c0_ref[bi, h],
                         jnp.broadcast_to(n0_ref[bi, h:h + 1, :], (HEAD_DIM, HEAD_DIM))], axis=0)
                    zero = jnp.zeros_like(cn)
                    states.append(jnp.concatenate([zero, cn] if i == 1 else [cn, zero], axis=1))
                    m_prevs.append(jnp.full(
                        (1, LANES), m0_all[layer, pl.program_id(0) * nb + bi, h], F32))
            cis = [bi * (t // mchunk) + c for c in range(t // mchunk)]
            rss = [slice(ci * mchunk, (ci + 1) * mchunk) for ci in cis]
            h_ts, states, m_prevs = _mlstm_pair(
                [c_k[rs, tile] for rs in rss],
                [[q[:, rs] for q in q_hs] for rs in rss], [[v[:, rs] for v in v_hs] for rs in rss],
                [[cum_b[h:h + 1, rs] for h in heads] for rs in rss],
                [[r_all[h:h + 1, rs] for h in heads] for rs in rss],
                [[cm_all[h:h + 1, rs] for h in heads] for rs in rss],
                [[b_last[ci][h:h + 1] for h in heads] for ci in cis],
                [[cm_last[ci][h:h + 1] for h in heads] for ci in cis],
                [[r[rs] for r in r_fulls] for rs in rss], states, m_prevs, causal_t)
            for c, rs in enumerate(rss):
                for i, h in enumerate(heads):
                    ht_ref[h * HEAD_DIM:(h + 1) * HEAD_DIM, rs] = h_ts[c][i]
            for i, h in enumerate(heads):
                if prompt:
                    mst_ref[bi, h] = states[i]
                mm_ref[bi, h:h + 1, :] = m_prevs[i]
                own = states[i][:, HEAD_DIM:] if i == 1 else states[i][:, :HEAD_DIM]
                cout_ref[bi, h] = own[:HEAD_DIM]
                nout_ref[bi, h:h + 1, :] = own[HEAD_DIM:HEAD_DIM + 1]
    col = dm.aw + dm.bw
    mixed_ref[:, col:col + dm.cw] = c_o * ht_ref[...].T

    x1 = x + _dot(mixed_ref[...].astype(BF16), wout_ref[...])
    x1_ref[...] = x1.reshape(nb, t, d_model)

    if prompt:
        for g in range(dm.kv):
            kbuf[0, g, 0:win, :] = kbuf[0, g, t:t + win, :]
            vbuf[0, g, 0:win, :] = vbuf[0, g, t:t + win, :]
        klast_ref[0] = k_rot[t - win:t]
        vlast_ref[0] = v_new[t - win:t]
    else:
        av_ref[...] = a_v.reshape(nb, t, dm.aw)
        knew_ref[...] = k_rot.reshape(nb, t, dm.kvw)
        vnew_ref[...] = v_new.reshape(nb, t, dm.kvw)


def _layer_spec(layer, shape):
    idx = (layer,) + (0,) * len(shape)
    return pl.BlockSpec((None,) + tuple(shape), lambda *_: idx, pipeline_mode=pl.Buffered(1))


def _mixer_call(dm, prompt, layer, x, tables, wts, caches=None):
    batch, seq, d_model = x.shape
    if prompt:
        nb, t, mchunk = 1, min(PROMPT_TILE, seq), min(MLSTM_CHUNK, seq)
        grid = (batch, seq // t)
        bmap = lambda b, s: (b, s, 0)
        smap = lambda b, s: (b, 0, 0)
        smap4 = lambda b, s: (b, 0, 0, 0)
        tmap = lambda b, s: (s, 0)
        sem = ("arbitrary", "arbitrary")
    else:
        nb, t, mchunk = min(SAMPLE_ROWS, batch), seq, seq
        grid = (batch // nb,)
        bmap = lambda b: (b, 0, 0)
        smap = bmap
        smap4 = lambda b: (b, 0, 0, 0)
        tmap = lambda b: (0, 0)
        sem = ("arbitrary",)
    assert seq % t == 0 and t % mchunk == 0 and t % CHUNK == 0 and (nb * t) % LANES == 0
    assert batch % nb == 0 and mchunk & (mchunk - 1) == 0 and mchunk <= LANES
    win = 2 * CHUNK

    in_specs = [
        pl.BlockSpec((nb, t, d_model), bmap),
        pl.BlockSpec((t, LANES), tmap), pl.BlockSpec((t, LANES), tmap), pl.BlockSpec((t, LANES), tmap),
        _layer_spec(layer, (1, d_model)),
        _layer_spec(layer, (d_model, dm.tm_width)),
        _layer_spec(layer, (dm.fm_rows, d_model)),
        _layer_spec(layer, (2 * SUBLANES, LANES)),
        _layer_spec(layer, (1, dm.aw)),
        _layer_spec(layer, (dm.ah, A_BLOCK, A_BLOCK)),
        _layer_spec(layer, (A_BLOCK, dm.aw)),
        pl.BlockSpec(memory_space=pltpu.SMEM),
        _layer_spec(layer, (d_model, d_model)),
    ]
    args = [x, *tables, wts["attn_norm"], wts["w_tm"], wts["w_fm"], wts["gate_b"], wts["gmlp_norm"],
            wts["gmlp_w"], wts["gmlp_bx"], wts["sinks"], wts["w_out"]]
    state_shapes = [jax.ShapeDtypeStruct((batch, dm.ch, HEAD_DIM, HEAD_DIM), F32),
                    jax.ShapeDtypeStruct((batch, dm.ch, HEAD_DIM), F32),
                    jax.ShapeDtypeStruct((batch, SUBLANES, LANES), F32)]
    state_specs = [pl.BlockSpec((nb, dm.ch, HEAD_DIM, HEAD_DIM), smap4),
                   pl.BlockSpec((nb, dm.ch, HEAD_DIM), smap),
                   pl.BlockSpec((nb, SUBLANES, LANES), smap)]
    if prompt:
        out_shape = [jax.ShapeDtypeStruct((batch, seq, d_model), F32),
                     jax.ShapeDtypeStruct((batch, win, dm.kvw), F32),
                     jax.ShapeDtypeStruct((batch, win, dm.kvw), F32)] + state_shapes
        out_specs = [pl.BlockSpec((nb, t, d_model), bmap),
                     pl.BlockSpec((1, win, dm.kvw), smap),
                     pl.BlockSpec((1, win, dm.kvw), smap)] + state_specs
    else:
        ck, cv, c0, n0, m0 = caches
        lmap = lambda b: (layer, b, 0, 0)
        in_specs += [pl.BlockSpec((None, nb, win, dm.kvw), lmap),
                     pl.BlockSpec((None, nb, win, dm.kvw), lmap),
                     pl.BlockSpec((None, nb, dm.ch, HEAD_DIM, HEAD_DIM), lambda b: (layer, b, 0, 0, 0)),
                     pl.BlockSpec((None, nb, dm.ch, HEAD_DIM), lmap),
                     pl.BlockSpec(memory_space=pltpu.SMEM)]
        args += [ck, cv, c0, n0, m0]
        out_shape = [jax.ShapeDtypeStruct((batch, seq, d_model), F32),
                     jax.ShapeDtypeStruct((batch, seq, dm.aw), F32),
                     jax.ShapeDtypeStruct((batch, seq, dm.kvw), F32),
                     jax.ShapeDtypeStruct((batch, seq, dm.kvw), F32)] + state_shapes
        out_specs = [pl.BlockSpec((nb, t, d_model), bmap),
                     pl.BlockSpec((nb, t, dm.aw), bmap),
                     pl.BlockSpec((nb, t, dm.kvw), bmap),
                     pl.BlockSpec((nb, t, dm.kvw), bmap)] + state_specs
    scratch = [pltpu.VMEM((nb, dm.kv, win + t, LANES), BF16),
               pltpu.VMEM((nb, dm.kv, win + t, 2 * LANES), BF16),
               pltpu.VMEM((nb * t, d_model), F32),
               pltpu.VMEM((dm.cw, nb * t), F32),
               pltpu.VMEM((nb, dm.ch, LANES, LANES), F32)]
    return pl.pallas_call(
        functools.partial(_mixer_kernel, dm, prompt, layer, nb, t, mchunk),
        grid=grid, in_specs=in_specs, out_specs=out_specs, out_shape=out_shape,
        scratch_shapes=scratch,
        compiler_params=pltpu.CompilerParams(dimension_semantics=sem,
                                             vmem_limit_bytes=VMEM_LIMIT_BYTES),
    )(*args)


def _ffn_kernel(prompt, final, nb, t, d_ff, *refs):
    if prompt:
        (x_ref, g_ref, wup_ref, cw_ref, cb_ref, wdn_ref, fin_ref,
         y_ref, cst_ref, hist_ref, gbuf) = refs
    else:
        (x_ref, g_ref, wup_ref, cw_ref, cb_ref, wdn_ref, fin_ref, cc_ref,
         y_ref, cst_ref, hist_ref, gbuf) = refs
    d_model = x_ref.shape[-1]
    rows = nb * t
    hist = CONV_W - 1
    assert hist == 2

    if prompt:
        @pl.when(pl.program_id(1) == 0)
        def _():
            hist_ref[...] = jnp.zeros(hist_ref.shape, F32)
    else:
        for bi in range(nb):
            hist_ref[bi] = jnp.concatenate(
                [jnp.zeros((SUBLANES - hist, 2 * d_ff), F32), cc_ref[bi]], axis=0)

    row8 = lax.broadcasted_iota(jnp.int32, (SUBLANES, FFN_COLS), 0)
    n_blocks = d_ff // FFN_COLS
    splits = set(range(FFN_DOWN_GROUP, n_blocks, FFN_DOWN_GROUP)) | {n_blocks}

    x = x_ref[...].reshape(rows, d_model)
    xn = _rmsnorm(x, g_ref[...]).astype(BF16)
    y = x
    k0 = 0
    dot_rows = min(FFN_DOT_ROWS, rows)

    def up_pair(j):
        outs = []
        for half in range(2):
            w = wup_ref[:, half * d_ff + j * FFN_COLS:half * d_ff + (j + 1) * FFN_COLS]
            outs.append(jnp.concatenate(
                [_dot(xn[r0:r0 + dot_rows], w) for r0 in range(0, rows, dot_rows)], axis=0))
        return outs

    for j in range(n_blocks):
        up_j = up_pair(j)
        conv = []
        for half in range(2):
            c0 = half * d_ff + j * FFN_COLS
            cs = slice(c0, c0 + FFN_COLS)
            up = up_j[half]
            w0, w1, w2 = cw_ref[0:1, cs], cw_ref[1:2, cs], cw_ref[2:3, cs]
            parts = []
            for bi in range(nb):
                ub = up[bi * t:(bi + 1) * t]
                prev = hist_ref[bi, :, cs]
                r1, r2 = pltpu.roll(ub, 1, 0), pltpu.roll(ub, 2, 0)
                h1 = jnp.where(row8 < 1, pltpu.roll(prev, 1, 0), r1[0:SUBLANES])
                h2 = jnp.where(row8 < 2, pltpu.roll(prev, 2, 0), r2[0:SUBLANES])
                s1 = jnp.concatenate([h1, r1[SUBLANES:]], axis=0)
                s2 = jnp.concatenate([h2, r2[SUBLANES:]], axis=0)
                parts.append(cb_ref[:, cs] + s2 * w0 + s1 * w1 + ub * w2)
                hist_ref[bi, :, cs] = ub[t - SUBLANES:t]
                cst_ref[bi, :, cs] = ub[t - hist:t]
            conv.append(parts[0] if nb == 1 else jnp.concatenate(parts, axis=0))
        gbuf[:, j * FFN_COLS:(j + 1) * FFN_COLS] = (
            _gelu2(conv[0].astype(BF16)) * conv[1].astype(BF16))
        if (j + 1) in splits:
            k1 = (j + 1) * FFN_COLS
            y = y + _dot(gbuf[:, k0:k1], wdn_ref[k0:k1, :])
            k0 = k1
    if final:
        y = _rmsnorm(y, fin_ref[...])
    y_ref[...] = y.reshape(nb, t, d_model)


def _ffn_call(prompt, final, layer, x, wts, final_norm, cache_conv=None):
    batch, seq, d_model = x.shape
    d_ff = wts["w_down"].shape[1]
    if prompt:
        nb, t = 1, min(FFN_TILE, seq)
        grid = (batch, seq // t)
        bmap = lambda b, s: (b, s, 0)
        smap = lambda b, s: (b, 0, 0)
        sem = ("arbitrary", "arbitrary")
    else:
        nb, t = min(SAMPLE_ROWS, batch), seq
        grid = (batch // nb,)
        bmap = lambda b: (b, 0, 0)
        smap = bmap
        sem = ("arbitrary",)
    assert seq % t == 0 and t % SUBLANES == 0 and t > SUBLANES and batch % nb == 0
    assert d_ff % FFN_COLS == 0
    hist = CONV_W - 1
    in_specs = [
        pl.BlockSpec((nb, t, d_model), bmap),
        _layer_spec(layer, (1, d_model)),
        _layer_spec(layer, (d_model, 2 * d_ff)),
        _layer_spec(layer, (CONV_W, 2 * d_ff)),
        _layer_spec(layer, (1, 2 * d_ff)),
        _layer_spec(layer, (d_ff, d_model)),
        _layer_spec(0, (1, d_model)),
    ]
    args = [x, wts["ffn_norm"], wts["w_up"], wts["conv_w"], wts["conv_b"], wts["w_down"],
            final_norm]
    if not prompt:
        in_specs.append(pl.BlockSpec((None, nb, hist, 2 * d_ff), lambda b: (layer, b, 0, 0)))
        args.append(cache_conv)
    out_shape = [jax.ShapeDtypeStruct((batch, seq, d_model), F32),
                 jax.ShapeDtypeStruct((batch, hist, 2 * d_ff), F32)]
    out_specs = [pl.BlockSpec((nb, t, d_model), bmap),
                 pl.BlockSpec((nb, hist, 2 * d_ff), smap)]
    scratch = [pltpu.VMEM((nb, SUBLANES, 2 * d_ff), F32),
               pltpu.VMEM((nb * t, d_ff), BF16)]
    return pl.pallas_call(
        functools.partial(_ffn_kernel, prompt, final, nb, t, d_ff),
        grid=grid, in_specs=in_specs, out_specs=out_specs, out_shape=out_shape,
        scratch_shapes=scratch,
        compiler_params=pltpu.CompilerParams(dimension_semantics=sem,
                                             vmem_limit_bytes=VMEM_LIMIT_BYTES),
    )(*args)


def _rope_tables(pos):
    half = ROT_DIM // 2
    inv_freq = ROPE_THETA ** (-jnp.arange(half, dtype=F32) / half)
    ang = pos.astype(F32)[:, None] * inv_freq[None, :]
    cos, sin = jnp.cos(ang), jnp.sin(ang)
    ones = jnp.ones((pos.shape[0], HEAD_DIM - ROT_DIM), F32)
    zeros = jnp.zeros((pos.shape[0], HEAD_DIM - ROT_DIM), F32)
    zh = jnp.zeros_like(sin)
    cos_t = jnp.concatenate([cos, cos, ones], axis=1)
    sin_a = jnp.concatenate([-sin, zh, zeros], axis=1)
    sin_b = jnp.concatenate([zh, sin, zeros], axis=1)
    rep = LANES // HEAD_DIM
    return tuple(jnp.tile(a, (1, rep)) for a in (cos_t, sin_a, sin_b))


def _prep_weights(dm, attn_norm, w_in, gmlp_norm, gmlp_w, gmlp_b, attn_sinks, mlstm_gate_b,
                  w_out, ffn_norm, w_up, conv_w, conv_b, w_down):
    def cols(off, width):
        return w_in[:, :, off:off + width]

    w_tm = jnp.concatenate([cols(dm.off_a_u, dm.aw), cols(dm.off_a_v, dm.aw), cols(dm.off_b_q, dm.bw),
                            cols(dm.off_b_k, dm.kvw), cols(dm.off_b_v, dm.kvw), cols(dm.off_c_k, dm.cw),
                            cols(dm.off_c_o, dm.cw)], axis=2)
    g_i, g_f = cols(dm.off_c_g, dm.ch), cols(dm.off_c_g + dm.ch, dm.ch)
    w_fm = jnp.swapaxes(
        jnp.concatenate([cols(dm.off_c_q, dm.cw), cols(dm.off_c_v, dm.cw), g_i, g_f, g_f, g_i], axis=2),
        1, 2)
    b_i, b_f = mlstm_gate_b[:, :dm.ch], mlstm_gate_b[:, dm.ch:]
    gate_b = jnp.concatenate([b_i, b_f, b_f, b_i], axis=1)
    depth = w_in.shape[0]
    return {
        "attn_norm": attn_norm[:, None, :],
        "w_tm": w_tm.astype(BF16),
        "w_fm": w_fm.astype(BF16),
        "gate_b": jnp.broadcast_to(gate_b[:, :, None], (depth, 2 * SUBLANES, LANES)),
        "gmlp_norm": gmlp_norm[:, None, :],
        "gmlp_w": gmlp_w,
        "gmlp_bx": jnp.repeat(jnp.swapaxes(gmlp_b, 1, 2), HEAD_DIM, axis=2),
        "sinks": attn_sinks,
        "w_out": w_out.astype(BF16),
        "ffn_norm": ffn_norm[:, None, :],
        "w_up": w_up.astype(BF16),
        "conv_w": conv_w,
        "conv_b": conv_b[:, None, :],
        "w_down": (0.5 * w_down).astype(BF16),
    }


def kernel(x_prompt, x_sample, cache_k, cache_v, state_C, state_n, state_m, cache_conv, attn_norm, w_in, gmlp_norm, gmlp_w, gmlp_b, attn_sinks, mlstm_gate_b, w_out, ffn_norm, w_up, conv_w, conv_b, w_down, final_norm):
    depth = w_in.shape[0]
    d_model = x_prompt.shape[-1]
    kv_heads = cache_k.shape[3]
    dm = _Dims(d_model, kv_heads)
    assert w_in.shape[-1] == dm.in_width
    bp, sp = x_prompt.shape[:2]
    bs, ts = x_sample.shape[:2]

    tab_p = _rope_tables(jnp.arange(sp, dtype=jnp.int32))
    tab_s = _rope_tables(PAST_LEN + jnp.arange(ts, dtype=jnp.int32))
    fin = final_norm[None, None, :]
    wts = _prep_weights(dm, attn_norm, w_in, gmlp_norm, gmlp_w, gmlp_b, attn_sinks, mlstm_gate_b,
                        w_out, ffn_norm, w_up, conv_w, conv_b, w_down)
    caches = (cache_k.reshape(depth, bs, -1, dm.kvw), cache_v.reshape(depth, bs, -1, dm.kvw),
              state_C, state_n, state_m)

    xp, xs = x_prompt, x_sample
    p_st, s_st = [], []
    for l in range(depth):
        last = l == depth - 1
        x1, k_last, v_last, c_p, n_p, mm = _mixer_call(dm, True, l, xp, tab_p, wts)
        xp, conv_p = _ffn_call(True, last, l, x1, wts, fin)
        p_st.append((k_last.reshape(bp, -1, kv_heads, HEAD_DIM), v_last.reshape(bp, -1, kv_heads, HEAD_DIM),
                     c_p, n_p, mm[:, :dm.ch, 0], conv_p))
        x1, a_v, k_new, v_new, c_s, n_s, mm = _mixer_call(dm, False, l, xs, tab_s, wts, caches)
        xs, conv_s = _ffn_call(False, last, l, x1, wts, fin, cache_conv)
        s_st.append((a_v, k_new.reshape(bs, ts, kv_heads, HEAD_DIM), v_new.reshape(bs, ts, kv_heads, HEAD_DIM),
                     c_s, n_s, mm[:, :dm.ch, 0], conv_s))

    def stk(states, i):
        return jnp.stack([s[i] for s in states], axis=0)

    return (xp, xs,
            stk(p_st, 0), stk(p_st, 1), stk(p_st, 2), stk(p_st, 3), stk(p_st, 4), stk(p_st, 5),
            stk(s_st, 0), stk(s_st, 1), stk(s_st, 2), stk(s_st, 3), stk(s_st, 4), stk(s_st, 5), stk(s_st, 6))
```
